```python
import math
import jax
import jax.numpy as jnp
from jax import lax
import numpy as np

D_MODEL = 1024
BATCH = 1
SEQ = 16384
DEPTH = 2
DEC_BATCH = 32
DEC_SEQ = 1
PAST_LEN = 16384
PAGE_SIZE = 128

N_MIXERS = 2
N_HGRN_LAYERS = (DEPTH + 1) // 2
N_MOBA_LAYERS = DEPTH // 2
HGRN_EXPAND = 128
HGRN_HEADS = D_MODEL // HGRN_EXPAND
HGRN_DK = HGRN_EXPAND
HGRN_DV = D_MODEL // HGRN_HEADS
HGRN_CHUNK = 64
MOBA_HEADS = 8
MOBA_HD = D_MODEL // MOBA_HEADS
MOBA_BLOCK = 256
MOBA_TOPK = 3
MOBA_Q_BLOCK = 32
REL_BUCKETS = 32
REL_MAX_DIST = 128
PEER_HEADS = 8
PEER_N_KEYS = 128
PEER_N_EXPERTS = PEER_N_KEYS * PEER_N_KEYS
PEER_D_KEY = 256
PEER_HALF = PEER_D_KEY // 2
PEER_TOPK = 16
PEER_TOKEN_BLOCK = 256
RMS_EPS = 1e-6

kernel_name = 'hgrn2_moba_peer_hybrid_step'


def rms_norm(x, g):
    xf = x.astype(jnp.float32)
    y = xf * lax.rsqrt(jnp.mean(xf * xf, axis=-1, keepdims=True) + RMS_EPS)
    return (y * g.astype(jnp.float32)).astype(x.dtype)


def t5_bucket(d):
    d = jnp.maximum(d, 0)
    max_exact = REL_BUCKETS // 2
    large = max_exact + (jnp.log(jnp.maximum(d, max_exact).astype(jnp.float32) / max_exact)
                         / math.log(REL_MAX_DIST / max_exact) * (REL_BUCKETS - max_exact)).astype(jnp.int32)
    large = jnp.minimum(large, REL_BUCKETS - 1)
    return jnp.where(d < max_exact, d, large)


def hgrn2_chunked(q, k, v, log_f, S0):
    B, T, H, DK = q.shape
    DV = v.shape[-1]
    c = min(HGRN_CHUNK, T)
    pad = (-T) % c
    q, k, v, log_f = (a.astype(jnp.float32) for a in (q, k, v, log_f))
    if pad:
        pw = ((0, 0), (0, pad), (0, 0), (0, 0))
        q, k, v, log_f = (jnp.pad(a, pw) for a in (q, k, v, log_f))
    n = (T + pad) // c

    def to_chunks(a):
        return a.reshape(B, n, c, H, a.shape[-1]).transpose(1, 0, 3, 2, 4)

    causal = jnp.tril(jnp.ones((c, c), bool))[:, :, None]

    def step(S, inp):
        qi, ki, vi, li = inp
        b = jnp.cumsum(li, axis=2)
        inter = jnp.einsum('bhtk,bhkv->bhtv', qi * jnp.exp(b), S)
        diff = b[:, :, :, None, :] - b[:, :, None, :, :]
        decay = jnp.exp(jnp.where(causal, diff, -jnp.inf))
        A = jnp.einsum('bhtk,bhsk,bhtsk->bhts', qi, ki, decay)
        intra = jnp.einsum('bhts,bhsv->bhtv', A, vi)
        bl = b[:, :, -1:, :]
        S_new = S * jnp.exp(bl[:, :, 0, :, None]) + jnp.einsum('bhsk,bhsv->bhkv', ki * jnp.exp(bl - b), vi)
        return S_new, inter + intra

    S, o = lax.scan(step, S0.astype(jnp.float32), tuple(to_chunks(a) for a in (q, k, v, log_f)))
    o = o.transpose(1, 0, 3, 2, 4).reshape(B, n * c, H, DV)[:, :T]
    return o, S.astype(S0.dtype)


def hgrn2_mixer(h, S0, w_in, lb, o_gain, w_out):
    B, T, _ = h.shape
    hk = HGRN_HEADS * HGRN_DK
    hv = HGRN_HEADS * HGRN_DV
    q, f_logit, i_in, g = jnp.split(h @ w_in, [hk, 2 * hk, 2 * hk + hv], axis=-1)
    f = lb + (1.0 - lb) * jax.nn.sigmoid(f_logit.astype(jnp.float32))
    log_f = jnp.log(f).reshape(B, T, HGRN_HEADS, HGRN_DK)
    kk = (1.0 - f).reshape(B, T, HGRN_HEADS, HGRN_DK)
    qq = jax.nn.silu(q).reshape(B, T, HGRN_HEADS, HGRN_DK)
    vv = i_in.reshape(B, T, HGRN_HEADS, HGRN_DV)
    o, S = hgrn2_chunked(qq, kk, vv, log_f, S0)
    o = rms_norm(o.astype(h.dtype), o_gain.reshape(HGRN_HEADS, HGRN_DV)).reshape(B, T, hv)
    o = o * jax.nn.silu(g)
    return o @ w_out, S


def moba_qkv(h, w_in, q_gain, k_gain):
    B, T, _ = h.shape
    q, k, v = jnp.split(h @ w_in, 3, axis=-1)
    q = rms_norm(q.reshape(B, T, MOBA_HEADS, MOBA_HD), q_gain)
    k = rms_norm(k.reshape(B, T, MOBA_HEADS, MOBA_HD), k_gain)
    return q, k, v.reshape(B, T, MOBA_HEADS, MOBA_HD)


def to_blocks(a):
    B, L, H, HD = a.shape
    pad = (-L) % MOBA_BLOCK
    a = jnp.pad(a, ((0, 0), (0, pad), (0, 0), (0, 0)))
    return a.reshape(B, (L + pad) // MOBA_BLOCK, MOBA_BLOCK, H, HD)


def moba_attend(q, q_pos, k_blocks, v_blocks, rel_bias):
    B, Tq, H, HD = q.shape
    NB = k_blocks.shape[1]
    k_mean = jnp.mean(k_blocks.astype(jnp.float32), axis=2)
    qb = min(MOBA_Q_BLOCK, Tq)
    pad = (-Tq) % qb
    nqb = (Tq + pad) // qb
    q_blocks = jnp.pad(q, ((0, 0), (0, pad), (0, 0), (0, 0))).reshape(B, nqb, qb, H, HD).transpose(1, 0, 2, 3, 4)
    pos_blocks = jnp.pad(q_pos, (0, pad), mode='edge').reshape(nqb, qb)
    n_sel = min(MOBA_TOPK, NB)
    b_i = jnp.arange(B)[:, None, None, None]
    h_i = jnp.arange(H)[None, :, None, None]
    offs = jnp.arange(MOBA_BLOCK, dtype=jnp.int32)
    blk_ids = jnp.arange(NB, dtype=jnp.int32)
    scale = MOBA_HD ** -0.5

    def one(args):
        qi, pi = args
        cur = pi // MOBA_BLOCK
        gate = jnp.einsum('bqhd,bnhd->bhqn', qi.astype(jnp.float32), k_mean)
        gate = jnp.where(blk_ids[None, None, None, :] < cur[None, None, :, None], gate, -jnp.inf)
        _, sel = lax.top_k(gate, n_sel)
        cur_b = jnp.broadcast_to(cur[None, None, :, None], (B, H, qb, 1))
        blk = jnp.concatenate([sel, cur_b], axis=-1)
        blk_ok = jnp.concatenate([sel < cur_b, jnp.ones_like(cur_b, dtype=bool)], axis=-1)
        k_sel = k_blocks[b_i, blk, :, h_i]
        v_sel = v_blocks[b_i, blk, :, h_i]
        rel = pi[None, None, :, None, None] - (blk[..., None] * MOBA_BLOCK + offs)
        ok = blk_ok[..., None] & (rel >= 0)
        bias = rel_bias[t5_bucket(rel), h_i[..., None]]
        logits = jnp.einsum('bqhd,bhqnrd->bhqnr', qi, k_sel).astype(jnp.float32) * scale + bias.astype(jnp.float32)
        logits = jnp.where(ok, logits, -jnp.inf).reshape(B, H, qb, -1)
        p = jax.nn.softmax(logits, axis=-1).astype(v_sel.dtype)
        return jnp.einsum('bhqm,bhqmd->bqhd', p, v_sel.reshape(B, H, qb, -1, HD))

    out = lax.map(one, (q_blocks, pos_blocks))
    return out.transpose(1, 0, 2, 3, 4).reshape(B, nqb * qb, H, HD)[:, :Tq]


def peer(h, w_q, keys1, keys2, u, v):
    B, T, D = h.shape
    n = B * T
    tb = min(PEER_TOKEN_BLOCK, n)
    pad = (-n) % tb
    xb = jnp.pad(h.reshape(n, D), ((0, pad), (0, 0))).reshape((n + pad) // tb, tb, D)

    def one(xi):
        q = (xi @ w_q).reshape(tb, PEER_HEADS, 2, PEER_HALF)
        s1 = jnp.einsum('thd,kd->thk', q[:, :, 0], keys1).astype(jnp.float32)
        s2 = jnp.einsum('thd,kd->thk', q[:, :, 1], keys2).astype(jnp.float32)
        v1, i1 = lax.top_k(s1, PEER_TOPK)
        v2, i2 = lax.top_k(s2, PEER_TOPK)
        cand = (v1[..., :, None] + v2[..., None, :]).reshape(tb, PEER_HEADS, PEER_TOPK * PEER_TOPK)
        cidx = (i1[..., :, None] * PEER_N_KEYS + i2[..., None, :]).reshape(tb, PEER_HEADS, PEER_TOPK * PEER_TOPK)
        sc, pos = lax.top_k(cand, PEER_TOPK)
        e = jnp.take_along_axis(cidx, pos, axis=-1)
        g = jax.nn.softmax(sc, axis=-1)
        act = jax.nn.gelu(jnp.einsum('td,thkd->thk', xi, u[e]).astype(jnp.float32), approximate=False)
        return jnp.einsum('thk,thkd->td', (g * act).astype(xi.dtype), v[e])

    out = lax.map(one, xb).reshape(n + pad, D)[:n]
    return out.reshape(B, T, D)


def setup_inputs(seed: int = 0) -> dict:
    key = jax.random.key(seed)
    ks = jax.random.split(key, 22)
    n_pages = PAST_LEN // PAGE_SIZE
    n_used = DEC_BATCH * n_pages
    n_pool = n_used + n_used // 4
    page_table = jax.random.permutation(ks[0], n_pool)[:n_used].reshape(DEC_BATCH, n_pages).astype(jnp.int32)
    hk = HGRN_HEADS * HGRN_DK
    hv = HGRN_HEADS * HGRN_DV
    md = MOBA_HEADS * MOBA_HD

    def nrm(k, shape, s):
        return jax.random.normal(k, shape, jnp.float32) * s

    return {
        'x_prompt': nrm(ks[1], (BATCH, SEQ, D_MODEL), 1.0),
        'x_sample': nrm(ks[2], (DEC_BATCH, DEC_SEQ, D_MODEL), 1.0),
        'state_hgrn': nrm(ks[3], (N_HGRN_LAYERS, DEC_BATCH, HGRN_HEADS, HGRN_DK, HGRN_DV), 0.5),
        'cache_k': nrm(ks[4], (N_MOBA_LAYERS, n_pool, PAGE_SIZE, MOBA_HEADS, MOBA_HD), 1.0),
        'cache_v': nrm(ks[5], (N_MOBA_LAYERS, n_pool, PAGE_SIZE, MOBA_HEADS, MOBA_HD), 1.0),
        'page_table': page_table,
        'norm_mix': 1.0 + nrm(ks[6], (DEPTH, D_MODEL), 0.02),
        'norm_ffn': 1.0 + nrm(ks[7], (DEPTH, D_MODEL), 0.02),
        'hgrn_w_in': nrm(ks[8], (N_HGRN_LAYERS, D_MODEL, 2 * hk + hv + D_MODEL), D_MODEL ** -0.5),
        'hgrn_lb': nrm(ks[9], (N_HGRN_LAYERS + 1, hk), 0.5),
        'hgrn_o_norm': 1.0 + nrm(ks[10], (N_HGRN_LAYERS, hv), 0.02),
        'hgrn_w_out': nrm(ks[11], (N_HGRN_LAYERS, hv, D_MODEL), hv ** -0.5),
        'moba_w_in': nrm(ks[12], (N_MOBA_LAYERS, D_MODEL, 3 * md), D_MODEL ** -0.5),
        'moba_q_norm': 1.0 + nrm(ks[13], (N_MOBA_LAYERS, MOBA_HD), 0.02),
        'moba_k_norm': 1.0 + nrm(ks[14], (N_MOBA_LAYERS, MOBA_HD), 0.02),
        'moba_w_out': nrm(ks[15], (N_MOBA_LAYERS, md, D_MODEL), md ** -0.5),
        'rel_bias': nrm(ks[16], (REL_BUCKETS, MOBA_HEADS), 0.2),
        'peer_w_q': nrm(ks[17], (DEPTH, D_MODEL, PEER_HEADS * PEER_D_KEY), D_MODEL ** -0.5),
        'peer_keys1': nrm(ks[18], (DEPTH, PEER_N_KEYS, PEER_HALF), PEER_HALF ** -0.5),
        'peer_keys2': nrm(ks[19], (DEPTH, PEER_N_KEYS, PEER_HALF), PEER_HALF ** -0.5),
        'peer_u': nrm(ks[20], (DEPTH, PEER_N_EXPERTS, D_MODEL), D_MODEL ** -0.5),
        'peer_v': nrm(ks[21], (DEPTH, PEER_N_EXPERTS, D_MODEL), (PEER_HEADS * PEER_TOPK) ** -0.5),
    }


def reference(x_prompt, x_sample, state_hgrn, cache_k, cache_v, page_table, norm_mix, norm_ffn,
              hgrn_w_in, hgrn_lb, hgrn_o_norm, hgrn_w_out, moba_w_in, moba_q_norm, moba_k_norm,
              moba_w_out, rel_bias, peer_w_q, peer_keys1, peer_keys2, peer_u, peer_v):
    lbs = jnp.cumsum(jax.nn.softmax(hgrn_lb.astype(jnp.float32), axis=0), axis=0)
    past_len = page_table.shape[1] * PAGE_SIZE
    pos_prompt = jnp.arange(x_prompt.shape[1], dtype=jnp.int32)
    pos_sample = past_len + jnp.arange(x_sample.shape[1], dtype=jnp.int32)
    xp, xs = x_prompt, x_sample
    st_p, st_s, kp_rows, vp_rows, ks_rows, vs_rows = [], [], [], [], [], []
    for layer in range(DEPTH):
        hp = rms_norm(xp, norm_mix[layer])
        hs = rms_norm(xs, norm_mix[layer])
        if layer % N_MIXERS == 0:
            a = layer // N_MIXERS
            S0p = jnp.zeros((xp.shape[0], HGRN_HEADS, HGRN_DK, HGRN_DV), xp.dtype)
            yp, Sp = hgrn2_mixer(hp, S0p, hgrn_w_in[a], lbs[a], hgrn_o_norm[a], hgrn_w_out[a])
            ys, Ss = hgrn2_mixer(hs, state_hgrn[a], hgrn_w_in[a], lbs[a], hgrn_o_norm[a], hgrn_w_out[a])
            st_p.append(Sp)
            st_s.append(Ss)
        else:
            b = layer // N_MIXERS
            qp, kp, vp = moba_qkv(hp, moba_w_in[b], moba_q_norm[b], moba_k_norm[b])
            qs, kn, vn = moba_qkv(hs, moba_w_in[b], moba_q_norm[b], moba_k_norm[b])
            op = moba_attend(qp, pos_prompt, to_blocks(kp), to_blocks(vp), rel_bias)
            nb_, ns_ = xs.shape[0], xs.shape[1]
            past_k = cache_k[b, page_table].reshape(nb_, past_len, MOBA_HEADS, MOBA_HD)
            past_v = cache_v[b, page_table].reshape(nb_, past_len, MOBA_HEADS, MOBA_HD)
            k_all = to_blocks(jnp.concatenate([past_k, kn.astype(past_k.dtype)], axis=1))
            v_all = to_blocks(jnp.concatenate([past_v, vn.astype(past_v.dtype)], axis=1))
            os_ = moba_attend(qs, pos_sample, k_all, v_all, rel_bias)
            yp = op.reshape(xp.shape[0], xp.shape[1], -1) @ moba_w_out[b]
            ys = os_.reshape(nb_, ns_, -1) @ moba_w_out[b]
            kp_rows.append(kp)
            vp_rows.append(vp)
            ks_rows.append(kn)
            vs_rows.append(vn)
        xp = xp + yp
        xs = xs + ys
        xp = xp + peer(rms_norm(xp, norm_ffn[layer]), peer_w_q[layer], peer_keys1[layer], peer_keys2[layer], peer_u[layer], peer_v[layer])
        xs = xs + peer(rms_norm(xs, norm_ffn[layer]), peer_w_q[layer], peer_keys1[layer], peer_keys2[layer], peer_u[layer], peer_v[layer])
    return (xp, xs, jnp.stack(st_p), jnp.stack(st_s), jnp.stack(kp_rows), jnp.stack(vp_rows), jnp.stack(ks_rows), jnp.stack(vs_rows))
```

```python
import functools
import math

import numpy as np
import jax
import jax.numpy as jnp
from jax import lax
from jax.experimental import pallas as pl
from jax.experimental.pallas import tpu as pltpu

F32 = jnp.float32
BF16 = jnp.bfloat16
I32 = jnp.int32

D_MODEL = 1024
HEAD_DIM = 128
N_HEADS = D_MODEL // HEAD_DIM
PAGE_SIZE = 128
HGRN_CHUNK = 64
MOBA_BLOCK = 256
MOBA_TOPK = 3
REL_BUCKETS = 32
REL_MAX_DIST = 128
PEER_HEADS = 8
PEER_N_KEYS = 128
PEER_HALF = 128
PEER_TOPK = 16
PEER_PAIRS = PEER_HEADS * PEER_TOPK
RMS_EPS = 1e-6

LANES = 128
SUBLANES = 8
V7X_VMEM_BYTES = 64 * 1024 * 1024
NEG_BIG = -1e30


def _vmem_limit(estimate_bytes):
    return int(min(max(2 * estimate_bytes, 16 * 1024 * 1024), V7X_VMEM_BYTES - 8 * 1024 * 1024))


def _params(sem, estimate_bytes):
    return pltpu.CompilerParams(dimension_semantics=sem, vmem_limit_bytes=_vmem_limit(estimate_bytes))


def _split2(x):
    hi = x.astype(BF16)
    lo = (x - hi.astype(F32)).astype(BF16)
    return hi, lo


def _split3(x):
    hi = x.astype(BF16)
    r = x - hi.astype(F32)
    mid = r.astype(BF16)
    lo = (r - mid.astype(F32)).astype(BF16)
    return hi, mid, lo


_NT = (((1,), (1,)), ((), ()))
_TN = (((0,), (0,)), ((), ()))


def _dot_nt(a, b):
    return lax.dot_general(a, b, _NT, preferred_element_type=F32)


def _dot_tn(a, b):
    return lax.dot_general(a, b, _TN, preferred_element_type=F32)


def _dot_nt_x3(a, b):
    ah, al = _split2(a)
    bh, bl = _split2(b)
    return _dot_nt(ah, bh) + _dot_nt(ah, bl) + _dot_nt(al, bh)


def _sigmoid(x):
    return 1.0 / (1.0 + jnp.exp(-x))


def _head_rms(y, gain):
    outs = []
    for h in range(y.shape[1] // HEAD_DIM):
        yh = y[:, h * HEAD_DIM:(h + 1) * HEAD_DIM]
        ms = jnp.mean(yh * yh, axis=-1, keepdims=True)
        outs.append(yh * lax.rsqrt(ms + RMS_EPS))
    return jnp.concatenate(outs, axis=-1) * gain


def _topk_rows(s, k):
    n = s.shape[0]
    row = lax.broadcasted_iota(I32, s.shape, 0)
    vals, idxs = [], []
    cur = s
    for _ in range(k):
        m = jnp.max(cur, axis=0, keepdims=True)
        idx = jnp.min(jnp.where(cur == m, row, n), axis=0, keepdims=True)
        vals.append(m)
        idxs.append(idx)
        cur = jnp.where(row == idx, -jnp.inf, cur)
    return jnp.concatenate(vals, axis=0), jnp.concatenate(idxs, axis=0)


def _norm_proj_kernel(x_ref, g_ref, w_ref, hg_ref, y_ref, *rest, n_head_norm_tiles, emit_h, emit_bf16):
    x = x_ref[...]
    ms = jnp.mean(x * x, axis=-1, keepdims=True)
    h = (x * lax.rsqrt(ms + RMS_EPS) * g_ref[...]).astype(BF16)
    y = jnp.dot(h, w_ref[...], preferred_element_type=F32)
    if n_head_norm_tiles:
        j = pl.program_id(1)
        y = jnp.where(j < n_head_norm_tiles, _head_rms(y, hg_ref[...]), y)
    y_ref[...] = y
    k = 0
    if emit_bf16:
        rest[k][...] = y.astype(BF16)
        k += 1
    if emit_h:
        rest[k][...] = h


def _norm_proj(x, gain, w_bf16, *, head_gain=None, n_head_norm_tiles=0, emit_h=False, emit_bf16=False, tn=1024):
    T, D = x.shape
    N = w_bf16.shape[1]
    tm = min(T, 1024)
    assert T % tm == 0 and N % tn == 0
    if head_gain is None:
        head_gain = jnp.ones((N,), F32)
    out_shape = [jax.ShapeDtypeStruct((T, N), F32)]
    out_specs = [pl.BlockSpec((tm, tn), lambda i, j: (i, j))]
    if emit_bf16:
        out_shape.append(jax.ShapeDtypeStruct((T, N), BF16))
        out_specs.append(pl.BlockSpec((tm, tn), lambda i, j: (i, j)))
    if emit_h:
        out_shape.append(jax.ShapeDtypeStruct((T, D), BF16))
        out_specs.append(pl.BlockSpec((tm, D), lambda i, j: (i, 0)))
    est = 2 * (tm * D * 4 + D * tn * 2 + tm * tn * 6 + tm * D * 2) + tm * tn * 8
    outs = pl.pallas_call(
        functools.partial(_norm_proj_kernel, n_head_norm_tiles=n_head_norm_tiles, emit_h=emit_h, emit_bf16=emit_bf16),
        grid=(T // tm, N // tn),
        in_specs=[
            pl.BlockSpec((tm, D), lambda i, j: (i, 0)),
            pl.BlockSpec((1, D), lambda i, j: (0, 0)),
            pl.BlockSpec((D, tn), lambda i, j: (0, j)),
            pl.BlockSpec((1, tn), lambda i, j: (0, j)),
        ],
        out_specs=out_specs,
        out_shape=out_shape,
        compiler_params=_params(("parallel", "arbitrary"), est),
        name="norm_proj",
    )(x, gain.reshape(1, D), w_bf16, head_gain.reshape(1, N))
    return outs


def _out_proj_kernel(o_ref, gate_ref, gain_ref, w_ref, x_ref, y_ref, *, gated):
    o = o_ref[...]
    if gated:
        g = gate_ref[...]
        o = _head_rms(o, gain_ref[...]) * (g * _sigmoid(g))
    y_ref[...] = x_ref[...] + jnp.dot(o.astype(BF16), w_ref[...], preferred_element_type=F32)


def _out_proj(o, w_bf16, x, *, gate_src=None, gate_col=0, gain=None):
    T, D = o.shape
    tm = min(T, 512)
    assert T % tm == 0
    gated = gate_src is not None
    if not gated:
        gate_src, gain = o, jnp.ones((D,), F32)
    est = 2 * (4 * tm * D * 4 + D * D * 2)
    return pl.pallas_call(
        functools.partial(_out_proj_kernel, gated=gated),
        grid=(T // tm,),
        in_specs=[
            pl.BlockSpec((tm, D), lambda i: (i, 0)),
            pl.BlockSpec((tm, D), lambda i: (i, gate_col)),
            pl.BlockSpec((1, D), lambda i: (0, 0)),
            pl.BlockSpec((D, D), lambda i: (0, 0)),
            pl.BlockSpec((tm, D), lambda i: (i, 0)),
        ],
        out_specs=pl.BlockSpec((tm, D), lambda i: (i, 0)),
        out_shape=jax.ShapeDtypeStruct((T, D), F32),
        compiler_params=_params(("parallel",), est),
        name="out_proj",
    )(o, gate_src, gain.reshape(1, D), w_bf16, x)


def _hgrn_prompt_kernel(q_ref, fl_ref, v_ref, lb_ref, o_ref, s_out_ref, st_scr, b_scr, k_scr):
    c = pl.program_id(0)
    C = q_ref.shape[0]

    @pl.when(c == 0)
    def _():
        st_scr[...] = jnp.zeros_like(st_scr)

    lb = lb_ref[...]
    f = lb + (1.0 - lb) * _sigmoid(fl_ref[...])
    logf = jnp.log(f)
    q = q_ref[...]
    qq = q * _sigmoid(q)
    tri = (lax.broadcasted_iota(I32, (C, C), 0) >= lax.broadcasted_iota(I32, (C, C), 1)).astype(BF16)
    p0, p1, p2 = _split3(logf)
    b = (jnp.dot(tri, p0, preferred_element_type=F32) + jnp.dot(tri, p1, preferred_element_type=F32)
         + jnp.dot(tri, p2, preferred_element_type=F32))
    b_scr[...] = b
    k_scr[...] = 1.0 - f
    row = lax.broadcasted_iota(I32, (C, HEAD_DIM), 0)
    col = lax.broadcasted_iota(I32, (C, C), 1)

    for h in range(N_HEADS):
        hs = slice(h * HEAD_DIM, (h + 1) * HEAD_DIM)
        bh = b[:, hs]
        qh = qq[:, hs]
        vh = v_ref[:, hs]
        kh = k_scr[:, hs]
        bl = bh[C - 1:C, :]

        def pair_cols(s8, a_mat):
            start = pl.multiple_of(s8 * SUBLANES, SUBLANES)
            b8 = b_scr[pl.ds(start, SUBLANES), hs]
            k8 = k_scr[pl.ds(start, SUBLANES), hs]
            for r in range(SUBLANES):
                s = start + r
                d = jnp.exp(jnp.where(row >= s, bh - b8[r:r + 1], -jnp.inf))
                a_col = jnp.sum(qh * k8[r:r + 1] * d, axis=-1, keepdims=True)
                a_mat = jnp.where(col == s, a_col, a_mat)
            return a_mat

        a_mat = lax.fori_loop(0, C // SUBLANES, pair_cols, jnp.zeros((C, C), F32))
        st = st_scr[h]
        inter = _dot_nt((qh * jnp.exp(bh)).astype(BF16), st.astype(BF16))
        intra = jnp.dot(a_mat.astype(BF16), vh.astype(BF16), preferred_element_type=F32)
        o_ref[:, hs] = inter + intra
        k_dec = (kh * jnp.exp(bl - bh)).astype(BF16)
        st_scr[h] = st * jnp.exp(bl) + _dot_tn(vh.astype(BF16), k_dec)

    @pl.when(c == pl.num_programs(0) - 1)
    def _():
        for h in range(N_HEADS):
            s_out_ref[h] = st_scr[h].T


def _hgrn_prompt(proj, lb):
    T = proj.shape[0]
    C = min(HGRN_CHUNK, T)
    assert T % C == 0 and C % SUBLANES == 0
    D = D_MODEL
    est = 2 * 4 * C * D * 4 + 3 * N_HEADS * HEAD_DIM * HEAD_DIM * 4 + 2 * C * D * 4
    return pl.pallas_call(
        _hgrn_prompt_kernel,
        grid=(T // C,),
        in_specs=[
            pl.BlockSpec((C, D), lambda c: (c, 0)),
            pl.BlockSpec((C, D), lambda c: (c, 1)),
            pl.BlockSpec((C, D), lambda c: (c, 2)),
            pl.BlockSpec((1, D), lambda c: (0, 0)),
        ],
        out_specs=[
            pl.BlockSpec((C, D), lambda c: (c, 0)),
            pl.BlockSpec((N_HEADS, HEAD_DIM, HEAD_DIM), lambda c: (0, 0, 0)),
        ],
        out_shape=[
            jax.ShapeDtypeStruct((T, D), F32),
            jax.ShapeDtypeStruct((N_HEADS, HEAD_DIM, HEAD_DIM), F32),
        ],
        scratch_shapes=[
            pltpu.VMEM((N_HEADS, HEAD_DIM, HEAD_DIM), F32),
            pltpu.VMEM((C, D), F32),
            pltpu.VMEM((C, D), F32),
        ],
        compiler_params=_params(("arbitrary",), est),
        name="hgrn_prompt",
    )(proj, proj, proj, lb.reshape(1, D))


def _hgrn_step_kernel(q_ref, fl_ref, lb_ref, v_ref, s_ref, o_ref, s_out_ref):
    lb = lb_ref[...]
    f = lb + (1.0 - lb) * _sigmoid(fl_ref[...])
    q = q_ref[...]
    qq = q * _sigmoid(q)
    s_new = s_ref[...] * f + (1.0 - f) * v_ref[...]
    s_out_ref[...] = s_new
    o_ref[...] = jnp.sum(s_new * qq, axis=1, keepdims=True)


def _hgrn_step(proj, lb, state):
    B = proj.shape[0]
    H, DK = N_HEADS, HEAD_DIM
    q = proj[:, :D_MODEL].reshape(B, H, DK, 1)
    fl = proj[:, D_MODEL:2 * D_MODEL].reshape(B, H, DK, 1)
    v = proj[:, 2 * D_MODEL:3 * D_MODEL].reshape(B, H, 1, DK)
    col_spec = pl.BlockSpec((None, H, DK, 1), lambda b: (b, 0, 0, 0))
    row_spec = pl.BlockSpec((None, H, 1, DK), lambda b: (b, 0, 0, 0))
    est = 2 * (3 * H * DK * LANES * 4 + 2 * H * DK * DK * 4)
    o, s_new = pl.pallas_call(
        _hgrn_step_kernel,
        grid=(B,),
        in_specs=[
            col_spec, col_spec,
            pl.BlockSpec((H, DK, 1), lambda b: (0, 0, 0)),
            row_spec,
            pl.BlockSpec((None, H, DK, DK), lambda b: (b, 0, 0, 0)),
        ],
        out_specs=[row_spec, pl.BlockSpec((None, H, DK, DK), lambda b: (b, 0, 0, 0))],
        out_shape=[jax.ShapeDtypeStruct((B, H, 1, DK), F32), jax.ShapeDtypeStruct(state.shape, F32)],
        compiler_params=_params(("parallel",), est),
        name="hgrn_step",
    )(q, fl, lb.reshape(H, DK, 1), v, state)
    return o.reshape(B, D_MODEL), s_new


def _rel_bucket_table():
    d = np.arange(REL_MAX_DIST + 1)
    max_exact = REL_BUCKETS // 2
    large = max_exact + (np.log(np.maximum(d, max_exact).astype(np.float32) / max_exact)
                         / math.log(REL_MAX_DIST / max_exact) * (REL_BUCKETS - max_exact)).astype(np.int32)
    large = np.minimum(large, REL_BUCKETS - 1)
    tab = np.where(d < max_exact, d, large).astype(np.int32)
    assert tab[REL_MAX_DIST] == REL_BUCKETS - 1
    return tab


def _block_mean_kernel(k_ref, o_ref):
    o_ref[...] = jnp.mean(k_ref[...], axis=0, keepdims=True)


def _block_mean(k_src, col_block, n_blocks):
    return pl.pallas_call(
        _block_mean_kernel,
        grid=(n_blocks,),
        in_specs=[pl.BlockSpec((MOBA_BLOCK, D_MODEL), lambda n: (n, col_block))],
        out_specs=pl.BlockSpec((None, 1, D_MODEL), lambda n: (n, 0, 0)),
        out_shape=jax.ShapeDtypeStruct((n_blocks, 1, D_MODEL), F32),
        compiler_params=_params(("parallel",), 4 * MOBA_BLOCK * D_MODEL * 4),
        name="moba_block_mean",
    )(k_src)


def _moba_prompt_kernel(q_ref, kmean_ref, k_ref, v_ref, bias_ref, far_ref, o_ref, sel_scr, m_scr, l_scr, acc_scr):
    i = pl.program_id(1)
    tq = q_ref.shape[0]
    nb = kmean_ref.shape[0]
    scale = HEAD_DIM ** -0.5
    q = q_ref[...]
    qb = q.astype(BF16)

    gate = _dot_nt_x3(kmean_ref[...], q)
    blk = lax.broadcasted_iota(I32, (nb, tq), 0)
    cur = jnp.where(blk < i, gate, -jnp.inf)
    sel = jnp.zeros((nb, tq), F32)
    for _ in range(MOBA_TOPK):
        m = jnp.max(cur, axis=0, keepdims=True)
        idx = jnp.min(jnp.where((cur == m) & (blk < i), blk, nb), axis=0, keepdims=True)
        hit = (blk == idx) & (cur > -jnp.inf)
        sel = jnp.where(hit, 1.0, sel)
        cur = jnp.where(blk == idx, -jnp.inf, cur)
    pad = sel_scr.shape[0] - nb
    sel_scr[...] = jnp.concatenate([sel, jnp.zeros((pad, tq), F32)], axis=0) if pad else sel
    sub = lax.broadcasted_iota(I32, (SUBLANES, tq), 0)

    def allowed(n):
        start = pl.multiple_of((n // SUBLANES) * SUBLANES, SUBLANES)
        rows = sel_scr[pl.ds(start, SUBLANES), :]
        return jnp.max(jnp.where(sub == n - start, rows, 0.0), axis=0, keepdims=True) > 0.0

    m_scr[...] = jnp.full_like(m_scr, NEG_BIG)
    l_scr[...] = jnp.zeros_like(l_scr)
    acc_scr[...] = jnp.zeros_like(acc_scr)

    def attend(n, bias, allow):
        start = pl.multiple_of(n * MOBA_BLOCK, MOBA_BLOCK)
        kn = k_ref[pl.ds(start, MOBA_BLOCK), :]
        vn = v_ref[pl.ds(start, MOBA_BLOCK), :]
        s = _dot_nt(kn, qb) * scale + bias
        m_old = m_scr[...]
        m_new = jnp.maximum(m_old, jnp.max(jnp.where(allow, s, NEG_BIG), axis=0, keepdims=True))
        alpha = jnp.exp(m_old - m_new)
        p = jnp.where(allow, jnp.exp(s - m_new), 0.0)
        l_scr[...] = alpha * l_scr[...] + jnp.sum(p, axis=0, keepdims=True)
        acc_scr[...] = alpha * acc_scr[...] + _dot_tn(vn, p.astype(BF16))
        m_scr[...] = m_new

    far = far_ref[0:1, 0:1]

    def far_block(n, carry):
        attend(n, far, allowed(n))
        return carry

    lax.fori_loop(0, jnp.maximum(i - 1, 0), far_block, 0)

    @pl.when(i >= 1)
    def _():
        attend(i - 1, bias_ref[1], allowed(i - 1))

    causal = lax.broadcasted_iota(I32, (MOBA_BLOCK, tq), 0) <= lax.broadcasted_iota(I32, (MOBA_BLOCK, tq), 1)
    attend(i, bias_ref[0], causal)
    o_ref[...] = (acc_scr[...] / l_scr[...]).T


def _moba_prompt(qkv, qkv_bf16, rel_bias):
    T = qkv.shape[0]
    assert T % MOBA_BLOCK == 0
    nb = T // MOBA_BLOCK
    H = N_HEADS
    kmean = _block_mean(qkv, 1, nb).reshape(nb, D_MODEL)
    tab = rel_bias[jnp.asarray(_rel_bucket_table())]
    kk = np.arange(MOBA_BLOCK)[:, None]
    qq = np.arange(MOBA_BLOCK)[None, :]
    d_own = np.clip(qq - kk, 0, REL_MAX_DIST)
    d_prev = np.clip(MOBA_BLOCK + qq - kk, 0, REL_MAX_DIST)
    near = jnp.stack([tab[d_own], tab[d_prev]], axis=0).transpose(3, 0, 1, 2)
    far = jnp.broadcast_to(tab[REL_MAX_DIST][:, None, None], (H, 8, LANES))
    est = 2 * (2 * T * HEAD_DIM * 2 + 2 * MOBA_BLOCK * MOBA_BLOCK * 4) + 16 * MOBA_BLOCK * MOBA_BLOCK * 4
    return pl.pallas_call(
        _moba_prompt_kernel,
        grid=(H, nb),
        in_specs=[
            pl.BlockSpec((MOBA_BLOCK, HEAD_DIM), lambda h, i: (i, h)),
            pl.BlockSpec((nb, HEAD_DIM), lambda h, i: (0, h)),
            pl.BlockSpec((T, HEAD_DIM), lambda h, i: (0, H + h)),
            pl.BlockSpec((T, HEAD_DIM), lambda h, i: (0, 2 * H + h)),
            pl.BlockSpec((None, 2, MOBA_BLOCK, MOBA_BLOCK), lambda h, i: (h, 0, 0, 0)),
            pl.BlockSpec((None, 8, LANES), lambda h, i: (h, 0, 0)),
        ],
        out_specs=pl.BlockSpec((MOBA_BLOCK, HEAD_DIM), lambda h, i: (i, h)),
        out_shape=jax.ShapeDtypeStruct((T, D_MODEL), F32),
        scratch_shapes=[
            pltpu.VMEM((-(-nb // SUBLANES) * SUBLANES, MOBA_BLOCK), F32),
            pltpu.VMEM((1, MOBA_BLOCK), F32),
            pltpu.VMEM((1, MOBA_BLOCK), F32),
            pltpu.VMEM((HEAD_DIM, MOBA_BLOCK), F32),
        ],
        compiler_params=_params(("parallel", "arbitrary"), est),
        name="moba_prompt",
    )(qkv, kmean, qkv_bf16, qkv_bf16, near, far)


def _paged_block_mean_kernel(pt_ref, k0_ref, k1_ref, o_ref):
    o_ref[...] = (jnp.sum(k0_ref[...], axis=0) + jnp.sum(k1_ref[...], axis=0)) * (1.0 / MOBA_BLOCK)


def _paged_block_mean(cache_k, page_table):
    B, n_pages = page_table.shape
    ppb = MOBA_BLOCK // PAGE_SIZE
    assert ppb == 2 and n_pages % ppb == 0
    nb = n_pages // ppb
    pt_flat = page_table.reshape(-1)
    page_shape = (None, PAGE_SIZE, N_HEADS, HEAD_DIM)
    grid_spec = pltpu.PrefetchScalarGridSpec(
        num_scalar_prefetch=1,
        grid=(B, nb),
        in_specs=[
            pl.BlockSpec(page_shape, lambda b, n, pt: (pt[b * n_pages + 2 * n], 0, 0, 0)),
            pl.BlockSpec(page_shape, lambda b, n, pt: (pt[b * n_pages + 2 * n + 1], 0, 0, 0)),
        ],
        out_specs=pl.BlockSpec((None, None, N_HEADS, HEAD_DIM), lambda b, n, pt: (b, n, 0, 0)),
    )
    return pl.pallas_call(
        _paged_block_mean_kernel,
        grid_spec=grid_spec,
        out_shape=jax.ShapeDtypeStruct((B, nb, N_HEADS, HEAD_DIM), F32),
        compiler_params=_params(("parallel", "arbitrary"), 4 * PAGE_SIZE * D_MODEL * 4),
        name="moba_paged_block_mean",
    )(pt_flat, cache_k, cache_k)


def _decode_gate_kernel(q_ref, kmean_ref, o_ref):
    nb = kmean_ref.shape[0]
    gate = jnp.sum(kmean_ref[...] * q_ref[...][None], axis=-1, keepdims=True)
    blk = lax.broadcasted_iota(I32, gate.shape, 0)
    cur = gate
    for r in range(MOBA_TOPK):
        m = jnp.max(cur, axis=0, keepdims=True)
        idx = jnp.min(jnp.where(cur == m, blk, nb), axis=0, keepdims=True)
        o_ref[r] = jnp.broadcast_to(idx[0], (N_HEADS, LANES))
        cur = jnp.where(blk == idx, -jnp.inf, cur)


def _decode_gate(q, kmean):
    B, nb = kmean.shape[:2]
    out = pl.pallas_call(
        _decode_gate_kernel,
        grid=(B,),
        in_specs=[
            pl.BlockSpec((None, N_HEADS, HEAD_DIM), lambda b: (b, 0, 0)),
            pl.BlockSpec((None, nb, N_HEADS, HEAD_DIM), lambda b: (b, 0, 0, 0)),
        ],
        out_specs=pl.BlockSpec((None, MOBA_TOPK, N_HEADS, LANES), lambda b: (b, 0, 0, 0)),
        out_shape=jax.ShapeDtypeStruct((B, MOBA_TOPK, N_HEADS, LANES), I32),
        compiler_params=_params(("parallel",), 4 * nb * N_HEADS * HEAD_DIM * 4),
        name="moba_decode_gate",
    )(q, kmean)
    return out[..., 0].transpose(0, 2, 1)


def _decode_attend_kernel(pages_ref, blocks_ref, q_ref, kn_ref, vn_ref, tab_ref, *refs, past_len):
    n_src = MOBA_TOPK * 2
    k_refs, v_refs, o_ref = refs[:n_src], refs[n_src:2 * n_src], refs[2 * n_src]
    b, h = pl.program_id(0), pl.program_id(1)
    scale = HEAD_DIM ** -0.5
    q = q_ref[...]
    tab = tab_ref[...]
    row = lax.broadcasted_iota(I32, (PAGE_SIZE, 1), 0)
    lane = lax.broadcasted_iota(I32, (PAGE_SIZE, 2 * LANES), 1)
    logits = []
    for j in range(n_src):
        blk = blocks_ref[(b * N_HEADS + h) * MOBA_TOPK + j // 2]
        rel = past_len - (blk * MOBA_BLOCK + (j % 2) * PAGE_SIZE + row)
        bias = jnp.sum(jnp.where(lane == jnp.minimum(rel, REL_MAX_DIST), tab, 0.0), axis=-1, keepdims=True)
        logits.append(jnp.sum(k_refs[j][...] * q, axis=-1, keepdims=True) * scale + bias)
    own = jnp.sum(kn_ref[...] * q, axis=-1, keepdims=True) * scale + tab[:, 0:1]
    m = own
    for s in logits:
        m = jnp.maximum(m, jnp.max(s, axis=0, keepdims=True))
    p_own = jnp.exp(own - m)
    l = p_own
    acc = p_own * vn_ref[...]
    for j in range(n_src):
        p = jnp.exp(logits[j] - m)
        l = l + jnp.sum(p, axis=0, keepdims=True)
        acc = acc + jnp.sum(p * v_refs[j][...], axis=0, keepdims=True)
    o_ref[...] = acc / l


def _moba_decode(q, k_new, v_new, cache_k, cache_v, page_table, rel_bias):
    B, n_pages = page_table.shape
    past_len = n_pages * PAGE_SIZE
    assert past_len % MOBA_BLOCK == 0 and past_len // MOBA_BLOCK >= MOBA_TOPK
    H = N_HEADS
    kmean = _paged_block_mean(cache_k, page_table)
    blocks = _decode_gate(q, kmean)
    pages = jnp.take_along_axis(
        page_table[:, None, :], (blocks[..., None] * 2 + jnp.arange(2, dtype=I32)).reshape(B, H, -1), axis=2)
    n_src = MOBA_TOPK * 2
    tab = rel_bias[jnp.asarray(_rel_bucket_table())].T
    tab = jnp.pad(tab, ((0, 0), (0, 2 * LANES - tab.shape[1]))).reshape(H, 1, 2 * LANES)
    n_pool = cache_k.shape[0]
    ck = cache_k.reshape(n_pool, PAGE_SIZE, D_MODEL)
    cv = cache_v.reshape(n_pool, PAGE_SIZE, D_MODEL)

    def page_spec(j):
        return pl.BlockSpec((None, PAGE_SIZE, HEAD_DIM),
                            lambda b, h, pg, bk: (pg[(b * H + h) * n_src + j], 0, h))

    vec_spec = pl.BlockSpec((None, None, 1, HEAD_DIM), lambda b, h, pg, bk: (b, h, 0, 0))
    grid_spec = pltpu.PrefetchScalarGridSpec(
        num_scalar_prefetch=2,
        grid=(B, H),
        in_specs=[vec_spec, vec_spec, vec_spec,
                  pl.BlockSpec((None, 1, 2 * LANES), lambda b, h, pg, bk: (h, 0, 0))]
                 + [page_spec(j) for j in range(n_src)] * 2,
        out_specs=vec_spec,
    )
    r4 = lambda a: a.reshape(B, H, 1, HEAD_DIM)
    out = pl.pallas_call(
        functools.partial(_decode_attend_kernel, past_len=past_len),
        grid_spec=grid_spec,
        out_shape=jax.ShapeDtypeStruct((B, H, 1, HEAD_DIM), F32),
        compiler_params=_params(("parallel", "arbitrary"), 4 * n_src * PAGE_SIZE * HEAD_DIM * 4),
        name="moba_decode_attend",
    )(pages.reshape(-1), blocks.reshape(-1), r4(q), r4(k_new), r4(v_new), tab,
      *([ck] * n_src), *([cv] * n_src))
    return out.reshape(B, D_MODEL)


def _peer_select_kernel(qq_ref, k1_ref, k2_ref, a_ref, b_ref, g_ref):
    k1 = k1_ref[...]
    k2 = k2_ref[...]
    a_rows, b_rows, g_rows = [], [], []
    for h in range(PEER_HEADS):
        q1 = qq_ref[:, (2 * h) * PEER_HALF:(2 * h + 1) * PEER_HALF]
        q2 = qq_ref[:, (2 * h + 1) * PEER_HALF:(2 * h + 2) * PEER_HALF]
        v1, i1 = _topk_rows(_dot_nt_x3(k1, q1), PEER_TOPK)
        v2, i2 = _topk_rows(_dot_nt_x3(k2, q2), PEER_TOPK)
        cand = jnp.concatenate([v1[r:r + 1] + v2 for r in range(PEER_TOPK)], axis=0)
        sc, pos = _topk_rows(cand, PEER_TOPK)
        r_k = pos >> 4
        c_k = pos & (PEER_TOPK - 1)
        a_k = jnp.zeros_like(pos)
        b_k = jnp.zeros_like(pos)
        for r in range(PEER_TOPK):
            a_k = jnp.where(r_k == r, i1[r:r + 1], a_k)
            b_k = jnp.where(c_k == r, i2[r:r + 1], b_k)
        e = jnp.exp(sc - sc[0:1])
        g_rows.append(e / jnp.sum(e, axis=0, keepdims=True))
        a_rows.append(a_k.astype(F32))
        b_rows.append(b_k.astype(F32))
    a_ref[...] = jnp.concatenate(a_rows, axis=0).T
    b_ref[...] = jnp.concatenate(b_rows, axis=0).T
    g_ref[...] = jnp.concatenate(g_rows, axis=0).T


def _peer_select(qq, keys1, keys2):
    T = qq.shape[0]
    tq = min(T, 256)
    assert T % tq == 0
    spec = pl.BlockSpec((tq, PEER_PAIRS), lambda i: (i, 0))
    key_spec = pl.BlockSpec((PEER_N_KEYS, PEER_HALF), lambda i: (0, 0))
    shape = jax.ShapeDtypeStruct((T, PEER_PAIRS), F32)
    return pl.pallas_call(
        _peer_select_kernel,
        grid=(T // tq,),
        in_specs=[pl.BlockSpec((tq, 2 * PEER_HEADS * PEER_HALF), lambda i: (i, 0)), key_spec, key_spec],
        out_specs=[spec, spec, spec],
        out_shape=[shape, shape, shape],
        compiler_params=_params(("parallel",), 4 * tq * 2048 * 4),
        name="peer_select",
    )(qq, keys1, keys2)


def _gelu_exact(x):
    return 0.5 * x * (1.0 + lax.erf(x * (2.0 ** -0.5)))


def _peer_mix_kernel(a_ref, b_ref, g_ref, h_ref, u_ref, v_ref, x_ref, y_ref, gmat_scr, acc_scr, *, groups):
    j = pl.program_id(1)
    tq = h_ref.shape[0]

    @pl.when(j == 0)
    def _():
        acc_scr[...] = jnp.zeros_like(acc_scr)
        key = lax.broadcasted_iota(I32, (PEER_N_KEYS, PEER_PAIRS), 0).astype(F32)

        def build(t8, carry):
            t0 = pl.multiple_of(t8 * SUBLANES, SUBLANES)
            a8 = a_ref[pl.ds(t0, SUBLANES), :]
            b8 = b_ref[pl.ds(t0, SUBLANES), :]
            g8 = g_ref[pl.ds(t0, SUBLANES), :]
            for r in range(SUBLANES):
                a_hot = (key == a8[r:r + 1]).astype(BF16)
                b_hot = jnp.where(key == b8[r:r + 1], g8[r:r + 1], 0.0).astype(BF16)
                start = pl.multiple_of((t0 + r) * PEER_N_KEYS, PEER_N_KEYS)
                gmat_scr[pl.ds(start, PEER_N_KEYS), :] = _dot_nt(a_hot, b_hot)
            return carry

        lax.fori_loop(0, tq // SUBLANES, build, 0)

    act = _gelu_exact(_dot_nt(h_ref[...], u_ref[...]))
    z = []
    for a in range(groups):
        w = gmat_scr[pl.ds(j * groups + a, tq, stride=PEER_N_KEYS), :]
        z.append((act[:, a * PEER_N_KEYS:(a + 1) * PEER_N_KEYS] * w).astype(BF16))
    acc_scr[...] += jnp.dot(jnp.concatenate(z, axis=-1), v_ref[...], preferred_element_type=F32)

    @pl.when(j == pl.num_programs(1) - 1)
    def _():
        y_ref[...] = x_ref[...] + acc_scr[...]


def _peer_mix(a_idx, b_idx, g, h_bf16, u_bf16, v_bf16, x):
    T, D = x.shape
    E = u_bf16.shape[0]
    tq = min(T, 256)
    groups = 4
    te = groups * PEER_N_KEYS
    assert T % tq == 0 and tq % SUBLANES == 0 and E % te == 0 and E == PEER_N_KEYS * PEER_N_KEYS
    pair_spec = pl.BlockSpec((tq, PEER_PAIRS), lambda i, j: (i, 0))
    tok_spec = pl.BlockSpec((tq, D), lambda i, j: (i, 0))
    est = tq * PEER_N_KEYS * PEER_N_KEYS * 4 + 2 * (2 * te * D * 2 + 3 * tq * D * 4) + 4 * tq * te * 4
    return pl.pallas_call(
        functools.partial(_peer_mix_kernel, groups=groups),
        grid=(T // tq, E // te),
        in_specs=[
            pair_spec, pair_spec, pair_spec, tok_spec,
            pl.BlockSpec((te, D), lambda i, j: (j, 0)),
            pl.BlockSpec((te, D), lambda i, j: (j, 0)),
            tok_spec,
        ],
        out_specs=tok_spec,
        out_shape=jax.ShapeDtypeStruct((T, D), F32),
        scratch_shapes=[
            pltpu.VMEM((tq * PEER_N_KEYS, PEER_N_KEYS), F32),
            pltpu.VMEM((tq, D), F32),
        ],
        compiler_params=_params(("parallel", "arbitrary"), est),
        name="peer_mix",
    )(a_idx, b_idx, g, h_bf16, u_bf16, v_bf16, x)


def _peer(x, norm_gain, w_q, keys1, keys2, u_bf16, v_bf16):
    qq, h_bf16 = _norm_proj(x, norm_gain, w_q, emit_h=True)
    a_idx, b_idx, g = _peer_select(qq, keys1, keys2)
    return _peer_mix(a_idx, b_idx, g, h_bf16, u_bf16, v_bf16, x)


def kernel(x_prompt, x_sample, state_hgrn, cache_k, cache_v, page_table, norm_mix, norm_ffn, hgrn_w_in, hgrn_lb,
           hgrn_o_norm, hgrn_w_out, moba_w_in, moba_q_norm, moba_k_norm, moba_w_out, rel_bias, peer_w_q,
           peer_keys1, peer_keys2, peer_u, peer_v):
    depth = norm_mix.shape[0]
    Bp, Tp, D = x_prompt.shape
    Bs, Ts, _ = x_sample.shape
    assert Bp == 1 and Ts == 1 and D == D_MODEL
    H = N_HEADS
    lbs = jnp.cumsum(jax.nn.softmax(hgrn_lb.astype(F32), axis=0), axis=0)
    xp = x_prompt.reshape(Tp, D)
    xs = x_sample.reshape(Bs, D)
    st_p, st_s, kp_rows, vp_rows, ks_rows, vs_rows = [], [], [], [], [], []
    for layer in range(depth):
        if layer % 2 == 0:
            a = layer // 2
            w_in = hgrn_w_in[a].astype(BF16)
            w_out = hgrn_w_out[a].astype(BF16)
            proj_p, = _norm_proj(xp, norm_mix[layer], w_in)
            proj_s, = _norm_proj(xs, norm_mix[layer], w_in)
            o_p, s_p = _hgrn_prompt(proj_p, lbs[a])
            o_s, s_s = _hgrn_step(proj_s, lbs[a], state_hgrn[a])
            xp = _out_proj(o_p, w_out, xp, gate_src=proj_p, gate_col=3, gain=hgrn_o_norm[a])
            xs = _out_proj(o_s, w_out, xs, gate_src=proj_s, gate_col=3, gain=hgrn_o_norm[a])
            st_p.append(s_p[None])
            st_s.append(s_s)
        else:
            b = layer // 2
            w_in = moba_w_in[b].astype(BF16)
            w_out = moba_w_out[b].astype(BF16)
            head_gain = jnp.concatenate([jnp.tile(moba_q_norm[b], H), jnp.tile(moba_k_norm[b], H), jnp.ones((D,), F32)])
            qkv_p, qkv_p16 = _norm_proj(xp, norm_mix[layer], w_in, head_gain=head_gain, n_head_norm_tiles=2,
                                        emit_bf16=True)
            qkv_s, = _norm_proj(xs, norm_mix[layer], w_in, head_gain=head_gain, n_head_norm_tiles=2)
            att_p = _moba_prompt(qkv_p, qkv_p16, rel_bias)
            q_s = qkv_s[:, :D].reshape(Bs, H, HEAD_DIM)
            k_s = qkv_s[:, D:2 * D].reshape(Bs, H, HEAD_DIM)
            v_s = qkv_s[:, 2 * D:].reshape(Bs, H, HEAD_DIM)
            att_s = _moba_decode(q_s, k_s, v_s, cache_k[b], cache_v[b], page_table, rel_bias)
            xp = _out_proj(att_p, w_out, xp)
            xs = _out_proj(att_s, w_out, xs)
            kp_rows.append(qkv_p[:, D:2 * D].reshape(Bp, Tp, H, HEAD_DIM))
            vp_rows.append(qkv_p[:, 2 * D:].reshape(Bp, Tp, H, HEAD_DIM))
            ks_rows.append(k_s.reshape(Bs, Ts, H, HEAD_DIM))
            vs_rows.append(v_s.reshape(Bs, Ts, H, HEAD_DIM))
        w_q = peer_w_q[layer].astype(BF16)
        u16 = peer_u[layer].astype(BF16)
        v16 = peer_v[layer].astype(BF16)
        xp = _peer(xp, norm_ffn[layer], w_q, peer_keys1[layer], peer_keys2[layer], u16, v16)
        xs = _peer(xs, norm_ffn[layer], w_q, peer_keys1[layer], peer_keys2[layer], u16, v16)
    return (xp.reshape(Bp, Tp, D), xs.reshape(Bs, Ts, D), jnp.stack(st_p), jnp.stack(st_s),
            jnp.stack(kp_rows), jnp.stack(vp_rows), jnp.stack(ks_rows), jnp.stack(vs_rows))
```

```python
import functools
import math

import numpy as np
import jax
import jax.numpy as jnp
from jax import lax
from jax.experimental import pallas as pl
from jax.experimental.pallas import tpu as pltpu

F32 = jnp.float32
BF16 = jnp.bfloat16
I32 = jnp.int32

D_MODEL = 1024
HEAD_DIM = 128
N_HEADS = D_MODEL // HEAD_DIM
PAGE_SIZE = 128
HGRN_CHUNK = 64
MOBA_BLOCK = 256
MOBA_TOPK = 3
REL_BUCKETS = 32
REL_MAX_DIST = 128
PEER_HEADS = 8
PEER_N_KEYS = 128
PEER_HALF = 128
PEER_TOPK = 16
PEER_PAIRS = PEER_HEADS * PEER_TOPK
RMS_EPS = 1e-6

LANES = 128
SUBLANES = 8
V7X_VMEM_BYTES = 64 * 1024 * 1024
NEG_BIG = -1e30
LOG2_E = 1.4426950408889634


def _vmem_limit(estimate_bytes):
    return int(min(max(2 * estimate_bytes, 16 * 1024 * 1024), V7X_VMEM_BYTES - 8 * 1024 * 1024))


def _params(sem, estimate_bytes):
    return pltpu.CompilerParams(dimension_semantics=sem, vmem_limit_bytes=_vmem_limit(estimate_bytes))


def _split2(x):
    hi = x.astype(BF16)
    lo = (x - hi.astype(F32)).astype(BF16)
    return hi, lo


def _split3(x):
    hi = x.astype(BF16)
    r = x - hi.astype(F32)
    mid = r.astype(BF16)
    lo = (r - mid.astype(F32)).astype(BF16)
    return hi, mid, lo


_NT = (((1,), (1,)), ((), ()))
_TN = (((0,), (0,)), ((), ()))


def _dot_nt(a, b):
    return lax.dot_general(a, b, _NT, preferred_element_type=F32)


def _dot_tn(a, b):
    return lax.dot_general(a, b, _TN, preferred_element_type=F32)


def _dot_nt_x3(a, b):
    ah, al = _split2(a)
    bh, bl = _split2(b)
    return _dot_nt(ah, bh) + _dot_nt(ah, bl) + _dot_nt(al, bh)


def _sigmoid(x):
    return 1.0 / (1.0 + jnp.exp(-x))


def _head_rms(y, gain):
    outs = []
    for h in range(y.shape[1] // HEAD_DIM):
        yh = y[:, h * HEAD_DIM:(h + 1) * HEAD_DIM]
        ms = jnp.mean(yh * yh, axis=-1, keepdims=True)
        outs.append(yh * lax.rsqrt(ms + RMS_EPS))
    return jnp.concatenate(outs, axis=-1) * gain


def _topk_rows(s, k, ids=None):
    if ids is None:
        ids = lax.broadcasted_iota(I32, s.shape, 0).astype(F32)
    vals, idxs = [], []
    cur = s
    for _ in range(k):
        m = jnp.max(cur, axis=0, keepdims=True)
        idx = jnp.min(jnp.where(cur == m, ids, jnp.inf), axis=0, keepdims=True)
        vals.append(m)
        idxs.append(idx)
        cur = jnp.where(ids == idx, -jnp.inf, cur)
    return jnp.concatenate(vals, axis=0), jnp.concatenate(idxs, axis=0)


def _norm_proj_kernel(x_ref, g_ref, w_ref, hg_ref, *rest, n_head_norm_tiles, emit_h, emit_bf16, emit_tail_t):
    rest = list(rest)
    wt_ref = rest.pop(0) if emit_tail_t else None
    y_ref = rest.pop(0)
    x = x_ref[...]
    ms = jnp.mean(x * x, axis=-1, keepdims=True)
    h = (x * lax.rsqrt(ms + RMS_EPS) * g_ref[...]).astype(BF16)
    y = jnp.dot(h, w_ref[...], preferred_element_type=F32)
    j = pl.program_id(1)
    if n_head_norm_tiles:
        y = jnp.where(j < n_head_norm_tiles, _head_rms(y, hg_ref[...]), y)
    y_ref[...] = y
    if emit_bf16:
        rest.pop(0)[...] = y.astype(BF16)
    if emit_h:
        rest.pop(0)[...] = h
    if emit_tail_t:
        yt_ref = rest.pop(0)

        @pl.when(j == pl.num_programs(1) - 1)
        def _():
            yt_ref[...] = _dot_nt(wt_ref[...], h).astype(BF16)


def _norm_proj(x, gain, w_bf16, *, head_gain=None, n_head_norm_tiles=0, emit_h=False, emit_bf16=False,
               emit_tail_t=False, tn=1024):
    T, D = x.shape
    N = w_bf16.shape[1]
    tm = min(T, 1024)
    assert T % tm == 0 and N % tn == 0
    n_tiles = N // tn
    if head_gain is None:
        head_gain = jnp.ones((N,), F32)
    operands = [x, gain.reshape(1, D), w_bf16, head_gain.reshape(1, N)]
    in_specs = [
        pl.BlockSpec((tm, D), lambda i, j: (i, 0)),
        pl.BlockSpec((1, D), lambda i, j: (0, 0)),
        pl.BlockSpec((D, tn), lambda i, j: (0, j)),
        pl.BlockSpec((1, tn), lambda i, j: (0, j)),
    ]
    out_shape = [jax.ShapeDtypeStruct((T, N), F32)]
    out_specs = [pl.BlockSpec((tm, tn), lambda i, j: (i, j))]
    if emit_bf16:
        out_shape.append(jax.ShapeDtypeStruct((T, N), BF16))
        out_specs.append(pl.BlockSpec((tm, tn), lambda i, j: (i, j)))
    if emit_h:
        out_shape.append(jax.ShapeDtypeStruct((T, D), BF16))
        out_specs.append(pl.BlockSpec((tm, D), lambda i, j: (i, 0)))
    if emit_tail_t:
        assert n_head_norm_tiles < n_tiles
        operands.append(w_bf16[:, N - tn:].T)
        in_specs.append(pl.BlockSpec((tn, D), lambda i, j: (0, 0)))
        out_shape.append(jax.ShapeDtypeStruct((tn, T), BF16))
        out_specs.append(pl.BlockSpec((tn, tm), lambda i, j: (0, i)))
    est = 2 * (tm * D * 4 + 2 * D * tn * 2 + tm * tn * 8 + tm * D * 2) + tm * tn * 8
    outs = pl.pallas_call(
        functools.partial(_norm_proj_kernel, n_head_norm_tiles=n_head_norm_tiles, emit_h=emit_h, emit_bf16=emit_bf16,
                          emit_tail_t=emit_tail_t),
        grid=(T // tm, n_tiles),
        in_specs=in_specs,
        out_specs=out_specs,
        out_shape=out_shape,
        compiler_params=_params(("parallel", "arbitrary"), est),
        name="norm_proj",
    )(*operands)
    return outs


def _out_proj_kernel(o_ref, gate_ref, gain_ref, w_ref, x_ref, y_ref, *, gated):
    o = o_ref[...]
    if gated:
        g = gate_ref[...]
        o = _head_rms(o, gain_ref[...]) * (g * _sigmoid(g))
    y_ref[...] = x_ref[...] + jnp.dot(o.astype(BF16), w_ref[...], preferred_element_type=F32)


def _out_proj(o, w_bf16, x, *, gate_src=None, gate_col=0, gain=None):
    T, D = o.shape
    tm = min(T, 512)
    assert T % tm == 0
    gated = gate_src is not None
    if not gated:
        gate_src, gain = o, jnp.ones((D,), F32)
    est = 2 * (4 * tm * D * 4 + D * D * 2)
    return pl.pallas_call(
        functools.partial(_out_proj_kernel, gated=gated),
        grid=(T // tm,),
        in_specs=[
            pl.BlockSpec((tm, D), lambda i: (i, 0)),
            pl.BlockSpec((tm, D), lambda i: (i, gate_col)),
            pl.BlockSpec((1, D), lambda i: (0, 0)),
            pl.BlockSpec((D, D), lambda i: (0, 0)),
            pl.BlockSpec((tm, D), lambda i: (i, 0)),
        ],
        out_specs=pl.BlockSpec((tm, D), lambda i: (i, 0)),
        out_shape=jax.ShapeDtypeStruct((T, D), F32),
        compiler_params=_params(("parallel",), est),
        name="out_proj",
    )(o, gate_src, gain.reshape(1, D), w_bf16, x)


HGRN_SAFE_DECAY = 80.0


def _hgrn_prompt_kernel(q_ref, fl_ref, v_ref, lb_ref, o_ref, s_out_ref, st_scr, b_scr, k_scr):
    c = pl.program_id(0)
    C = q_ref.shape[0]

    @pl.when(c == 0)
    def _():
        st_scr[...] = jnp.zeros_like(st_scr)

    lb = lb_ref[...]
    f = lb + (1.0 - lb) * _sigmoid(fl_ref[...])
    logf = jnp.log(f)
    q = q_ref[...]
    qq = q * _sigmoid(q)
    tri = (lax.broadcasted_iota(I32, (C, C), 0) >= lax.broadcasted_iota(I32, (C, C), 1)).astype(BF16)
    p0, p1, p2 = _split3(logf)
    b = (jnp.dot(tri, p0, preferred_element_type=F32) + jnp.dot(tri, p1, preferred_element_type=F32)
         + jnp.dot(tri, p2, preferred_element_type=F32))
    b_scr[...] = b
    k_scr[...] = 1.0 - f
    row = lax.broadcasted_iota(I32, (C, HEAD_DIM), 0)
    col = lax.broadcasted_iota(I32, (C, C), 1)
    causal = lax.broadcasted_iota(I32, (C, C), 0) >= col

    def head_step(h, hs, a_mat):
        bh = b_scr[:, hs]
        vh = v_ref[:, hs].astype(BF16)
        bl = bh[C - 1:C, :]
        qh = q_ref[:, hs]
        qh = qh * _sigmoid(qh)
        st = st_scr[h]
        inter = _dot_nt((qh * jnp.exp(bh)).astype(BF16), st.astype(BF16))
        intra = jnp.dot(a_mat.astype(BF16), vh, preferred_element_type=F32)
        o_ref[:, hs] = inter + intra
        k_dec = (k_scr[:, hs] * jnp.exp(bl - bh)).astype(BF16)
        st_scr[h] = st * jnp.exp(bl) + _dot_tn(vh, k_dec)

    safe = jnp.min(b[C - 1:C, :]) >= -HGRN_SAFE_DECAY

    @pl.when(safe)
    def _():
        for h in range(N_HEADS):
            hs = slice(h * HEAD_DIM, (h + 1) * HEAD_DIM)
            bh = b[:, hs]
            half = 0.5 * bh[C - 1:C, :]
            q_dec = (qq[:, hs] * jnp.exp(bh - half)).astype(BF16)
            k_inc = (k_scr[:, hs] * jnp.exp(half - bh)).astype(BF16)
            head_step(h, hs, jnp.where(causal, _dot_nt(q_dec, k_inc), 0.0))

    @pl.when(jnp.logical_not(safe))
    def _():
        for h in range(N_HEADS):
            hs = slice(h * HEAD_DIM, (h + 1) * HEAD_DIM)
            bh = b[:, hs]
            qh = qq[:, hs]

            def pair_cols(s8, a_mat):
                start = pl.multiple_of(s8 * SUBLANES, SUBLANES)
                b8 = b_scr[pl.ds(start, SUBLANES), hs]
                k8 = k_scr[pl.ds(start, SUBLANES), hs]
                for r in range(SUBLANES):
                    s = start + r
                    d = jnp.exp(jnp.where(row >= s, bh - b8[r:r + 1], -jnp.inf))
                    a_col = jnp.sum(qh * k8[r:r + 1] * d, axis=-1, keepdims=True)
                    a_mat = jnp.where(col == s, a_col, a_mat)
                return a_mat

            head_step(h, hs, lax.fori_loop(0, C // SUBLANES, pair_cols, jnp.zeros((C, C), F32)))

    @pl.when(c == pl.num_programs(0) - 1)
    def _():
        for h in range(N_HEADS):
            s_out_ref[h] = st_scr[h].T


def _hgrn_prompt(proj, lb):
    T = proj.shape[0]
    C = min(HGRN_CHUNK, T)
    assert T % C == 0 and C % SUBLANES == 0
    D = D_MODEL
    est = 2 * 4 * C * D * 4 + 3 * N_HEADS * HEAD_DIM * HEAD_DIM * 4 + 2 * C * D * 4
    return pl.pallas_call(
        _hgrn_prompt_kernel,
        grid=(T // C,),
        in_specs=[
            pl.BlockSpec((C, D), lambda c: (c, 0)),
            pl.BlockSpec((C, D), lambda c: (c, 1)),
            pl.BlockSpec((C, D), lambda c: (c, 2)),
            pl.BlockSpec((1, D), lambda c: (0, 0)),
        ],
        out_specs=[
            pl.BlockSpec((C, D), lambda c: (c, 0)),
            pl.BlockSpec((N_HEADS, HEAD_DIM, HEAD_DIM), lambda c: (0, 0, 0)),
        ],
        out_shape=[
            jax.ShapeDtypeStruct((T, D), F32),
            jax.ShapeDtypeStruct((N_HEADS, HEAD_DIM, HEAD_DIM), F32),
        ],
        scratch_shapes=[
            pltpu.VMEM((N_HEADS, HEAD_DIM, HEAD_DIM), F32),
            pltpu.VMEM((C, D), F32),
            pltpu.VMEM((C, D), F32),
        ],
        compiler_params=_params(("arbitrary",), est),
        name="hgrn_prompt",
    )(proj, proj, proj, lb.reshape(1, D))


def _hgrn_step_kernel(q_ref, fl_ref, lb_ref, v_ref, s_ref, o_ref, s_out_ref):
    lb = lb_ref[...]
    f = lb + (1.0 - lb) * _sigmoid(fl_ref[...])
    q = q_ref[...]
    qq = q * _sigmoid(q)
    s_new = s_ref[...] * f + (1.0 - f) * v_ref[...]
    s_out_ref[...] = s_new
    o_ref[...] = jnp.sum(s_new * qq, axis=1, keepdims=True)


def _hgrn_step(proj, lb, state):
    B = proj.shape[0]
    H, DK = N_HEADS, HEAD_DIM
    q = proj[:, :D_MODEL].reshape(B, H, DK, 1)
    fl = proj[:, D_MODEL:2 * D_MODEL].reshape(B, H, DK, 1)
    v = proj[:, 2 * D_MODEL:3 * D_MODEL].reshape(B, H, 1, DK)
    col_spec = pl.BlockSpec((None, H, DK, 1), lambda b: (b, 0, 0, 0))
    row_spec = pl.BlockSpec((None, H, 1, DK), lambda b: (b, 0, 0, 0))
    est = 2 * (3 * H * DK * LANES * 4 + 2 * H * DK * DK * 4)
    o, s_new = pl.pallas_call(
        _hgrn_step_kernel,
        grid=(B,),
        in_specs=[
            col_spec, col_spec,
            pl.BlockSpec((H, DK, 1), lambda b: (0, 0, 0)),
            row_spec,
            pl.BlockSpec((None, H, DK, DK), lambda b: (b, 0, 0, 0)),
        ],
        out_specs=[row_spec, pl.BlockSpec((None, H, DK, DK), lambda b: (b, 0, 0, 0))],
        out_shape=[jax.ShapeDtypeStruct((B, H, 1, DK), F32), jax.ShapeDtypeStruct(state.shape, F32)],
        compiler_params=_params(("parallel",), est),
        name="hgrn_step",
    )(q, fl, lb.reshape(H, DK, 1), v, state)
    return o.reshape(B, D_MODEL), s_new


def _rel_bucket_table():
    d = np.arange(REL_MAX_DIST + 1)
    max_exact = REL_BUCKETS // 2
    large = max_exact + (np.log(np.maximum(d, max_exact).astype(np.float32) / max_exact)
                         / math.log(REL_MAX_DIST / max_exact) * (REL_BUCKETS - max_exact)).astype(np.int32)
    large = np.minimum(large, REL_BUCKETS - 1)
    tab = np.where(d < max_exact, d, large).astype(np.int32)
    assert tab[REL_MAX_DIST] == REL_BUCKETS - 1
    return tab


def _block_mean_kernel(k_ref, o_ref):
    o_ref[...] = jnp.mean(k_ref[...], axis=0, keepdims=True)


def _block_mean(k_src, col_block, n_blocks):
    return pl.pallas_call(
        _block_mean_kernel,
        grid=(n_blocks,),
        in_specs=[pl.BlockSpec((MOBA_BLOCK, D_MODEL), lambda n: (n, col_block))],
        out_specs=pl.BlockSpec((None, 1, D_MODEL), lambda n: (n, 0, 0)),
        out_shape=jax.ShapeDtypeStruct((n_blocks, 1, D_MODEL), F32),
        compiler_params=_params(("parallel",), 4 * MOBA_BLOCK * D_MODEL * 4),
        name="moba_block_mean",
    )(k_src)


MOBA_FAR_GROUP = 4


def _moba_prompt_kernel(q_ref, kmean_ref, k_ref, vt_ref, bias_ref, far_ref, o_ref, sel_scr, m_scr, l_scr, acc_scr,
                        *, group):
    i = pl.program_id(1)
    tq = q_ref.shape[0]
    nb = kmean_ref.shape[0]
    q = q_ref[...]
    qs = (q * (HEAD_DIM ** -0.5 * LOG2_E)).astype(BF16)

    gate = _dot_nt_x3(kmean_ref[...], q)
    blk = lax.broadcasted_iota(I32, (nb, tq), 0)
    cur = jnp.where(blk < i, gate, -jnp.inf)
    sel = jnp.zeros((nb, tq), F32)
    for _ in range(MOBA_TOPK):
        m = jnp.max(cur, axis=0, keepdims=True)
        idx = jnp.min(jnp.where((cur == m) & (blk < i), blk, nb), axis=0, keepdims=True)
        hit = (blk == idx) & (cur > -jnp.inf)
        sel = jnp.where(hit, 1.0, sel)
        cur = jnp.where(blk == idx, -jnp.inf, cur)
    pad = sel_scr.shape[0] - nb
    sel_scr[...] = jnp.concatenate([sel, jnp.zeros((pad, tq), F32)], axis=0) if pad else sel
    sub = lax.broadcasted_iota(I32, (SUBLANES, tq), 0)

    def allowed(n):
        start = pl.multiple_of((n // SUBLANES) * SUBLANES, SUBLANES)
        rows = sel_scr[pl.ds(start, SUBLANES), :]
        return jnp.max(jnp.where(sub == n - start, rows, 0.0), axis=0, keepdims=True) > 0.0

    m_scr[...] = jnp.full_like(m_scr, NEG_BIG)
    l_scr[...] = jnp.zeros_like(l_scr)
    acc_scr[...] = jnp.zeros_like(acc_scr)

    def attend(n, bias, allow):
        start = pl.multiple_of(n * MOBA_BLOCK, MOBA_BLOCK)
        s = _dot_nt(k_ref[pl.ds(start, MOBA_BLOCK), :], qs) + bias
        m_old = m_scr[...]
        m_new = jnp.maximum(m_old, jnp.max(jnp.where(allow, s, NEG_BIG), axis=0, keepdims=True))
        alpha = jnp.exp2(m_old - m_new)
        p = jnp.where(allow, jnp.exp2(s - m_new), 0.0)
        l_scr[...] = alpha * l_scr[...] + jnp.sum(p, axis=0, keepdims=True)
        pv = jnp.dot(vt_ref[:, pl.ds(start, MOBA_BLOCK)], p.astype(BF16), preferred_element_type=F32)
        acc_scr[...] = alpha * acc_scr[...] + pv
        m_scr[...] = m_new

    far = far_ref[0:1, 0:1]

    def far_group(ng, carry):
        n0 = ng * group
        start = pl.multiple_of(n0 * MOBA_BLOCK, group * MOBA_BLOCK)
        s = _dot_nt(k_ref[pl.ds(start, group * MOBA_BLOCK), :], qs)
        parts = [s[g * MOBA_BLOCK:(g + 1) * MOBA_BLOCK] for g in range(group)]
        allow = [allowed(n0 + g) & (n0 + g < i - 1) for g in range(group)]
        mx = jnp.where(allow[0], jnp.max(parts[0], axis=0, keepdims=True), NEG_BIG)
        for g in range(1, group):
            mx = jnp.maximum(mx, jnp.where(allow[g], jnp.max(parts[g], axis=0, keepdims=True), NEG_BIG))
        m_old = m_scr[...]
        m_new = jnp.maximum(m_old, mx + far)
        alpha = jnp.exp2(m_old - m_new)
        shift = m_new - far
        l_new = alpha * l_scr[...]
        acc = alpha * acc_scr[...]
        for g in range(group):
            p = jnp.exp2(parts[g] - shift)
            l_new = l_new + jnp.where(allow[g], jnp.sum(p, axis=0, keepdims=True), 0.0)
            start_g = pl.multiple_of(start + g * MOBA_BLOCK, MOBA_BLOCK)
            pv = jnp.dot(vt_ref[:, pl.ds(start_g, MOBA_BLOCK)], p.astype(BF16), preferred_element_type=F32)
            acc = acc + jnp.where(allow[g], pv, 0.0)
        l_scr[...] = l_new
        acc_scr[...] = acc
        m_scr[...] = m_new
        return carry

    lax.fori_loop(0, (i + group - 2) // group, far_group, 0)

    @pl.when(i >= 1)
    def _():
        attend(i - 1, bias_ref[1], allowed(i - 1))

    causal = lax.broadcasted_iota(I32, (MOBA_BLOCK, tq), 0) <= lax.broadcasted_iota(I32, (MOBA_BLOCK, tq), 1)
    attend(i, bias_ref[0], causal)
    o_ref[...] = (acc_scr[...] / l_scr[...]).T


def _moba_prompt(qkv, qkv_bf16, vt_bf16, rel_bias):
    T = qkv.shape[0]
    assert T % MOBA_BLOCK == 0
    nb = T // MOBA_BLOCK
    H = N_HEADS
    kmean = _block_mean(qkv, 1, nb).reshape(nb, D_MODEL)
    tab = rel_bias[jnp.asarray(_rel_bucket_table())] * LOG2_E
    n = MOBA_BLOCK
    e = np.arange(2 * n)
    e_signed = np.where(e < n, e, e - 2 * n)
    w_own = tab[np.clip(e_signed, 0, REL_MAX_DIST)]
    w_prev = tab[np.clip(n + e_signed, 0, REL_MAX_DIST)]

    def toeplitz(w):
        return jnp.tile(w, n)[:n * (2 * n - 1)].reshape(n, 2 * n - 1)[:, :n]

    near = jnp.stack([jax.vmap(toeplitz, in_axes=1)(w_own), jax.vmap(toeplitz, in_axes=1)(w_prev)], axis=1)
    far = jnp.broadcast_to(tab[REL_MAX_DIST][:, None, None], (H, 8, LANES))
    est = 2 * (2 * T * HEAD_DIM * 2 + 2 * MOBA_BLOCK * MOBA_BLOCK * 4) + 16 * MOBA_BLOCK * MOBA_BLOCK * 4
    return pl.pallas_call(
        functools.partial(_moba_prompt_kernel, group=math.gcd(nb, MOBA_FAR_GROUP)),
        grid=(H, nb),
        in_specs=[
            pl.BlockSpec((MOBA_BLOCK, HEAD_DIM), lambda h, i: (i, h)),
            pl.BlockSpec((nb, HEAD_DIM), lambda h, i: (0, h)),
            pl.BlockSpec((T, HEAD_DIM), lambda h, i: (0, H + h)),
            pl.BlockSpec((HEAD_DIM, T), lambda h, i: (h, 0)),
            pl.BlockSpec((None, 2, MOBA_BLOCK, MOBA_BLOCK), lambda h, i: (h, 0, 0, 0)),
            pl.BlockSpec((None, 8, LANES), lambda h, i: (h, 0, 0)),
        ],
        out_specs=pl.BlockSpec((MOBA_BLOCK, HEAD_DIM), lambda h, i: (i, h)),
        out_shape=jax.ShapeDtypeStruct((T, D_MODEL), F32),
        scratch_shapes=[
            pltpu.VMEM((-(-nb // SUBLANES) * SUBLANES, MOBA_BLOCK), F32),
            pltpu.VMEM((1, MOBA_BLOCK), F32),
            pltpu.VMEM((1, MOBA_BLOCK), F32),
            pltpu.VMEM((HEAD_DIM, MOBA_BLOCK), F32),
        ],
        compiler_params=_params(("parallel", "arbitrary"), est),
        name="moba_prompt",
    )(qkv, kmean, qkv_bf16, vt_bf16, near, far)


PAGES_PER_BLOCK = MOBA_BLOCK // PAGE_SIZE
MEAN_BLOCKS_PER_STEP = 4


def _paged_block_mean_kernel(pt_ref, *refs):
    page_refs, o_ref = refs[:-1], refs[-1]
    for n in range(len(page_refs) // PAGES_PER_BLOCK):
        tot = jnp.sum(page_refs[n * PAGES_PER_BLOCK][...], axis=0)
        for p in range(1, PAGES_PER_BLOCK):
            tot = tot + jnp.sum(page_refs[n * PAGES_PER_BLOCK + p][...], axis=0)
        o_ref[n] = tot * (1.0 / MOBA_BLOCK)


def _paged_block_mean(cache_k, layer, page_table):
    B, n_pages = page_table.shape
    assert n_pages % PAGES_PER_BLOCK == 0
    nb = n_pages // PAGES_PER_BLOCK
    bps = math.gcd(nb, MEAN_BLOCKS_PER_STEP)
    pps = bps * PAGES_PER_BLOCK
    page_shape = (None, None, PAGE_SIZE, N_HEADS, HEAD_DIM)

    def page_spec(p):
        return pl.BlockSpec(page_shape, lambda b, n, pt: (layer, pt[b * n_pages + n * pps + p], 0, 0, 0))

    grid_spec = pltpu.PrefetchScalarGridSpec(
        num_scalar_prefetch=1,
        grid=(B, nb // bps),
        in_specs=[page_spec(p) for p in range(pps)],
        out_specs=pl.BlockSpec((None, bps, N_HEADS, HEAD_DIM), lambda b, n, pt: (b, n, 0, 0)),
    )
    return pl.pallas_call(
        _paged_block_mean_kernel,
        grid_spec=grid_spec,
        out_shape=jax.ShapeDtypeStruct((B, nb, N_HEADS, HEAD_DIM), F32),
        compiler_params=_params(("parallel", "arbitrary"), 2 * pps * PAGE_SIZE * D_MODEL * 4),
        name="moba_paged_block_mean",
    )(page_table.reshape(-1), *([cache_k] * pps))


def _decode_gate_kernel(q_ref, kmean_ref, o_ref):
    nb = kmean_ref.shape[0]
    gate = jnp.sum(kmean_ref[...] * q_ref[...][None], axis=-1, keepdims=True)
    blk = lax.broadcasted_iota(I32, gate.shape, 0)
    cur = gate
    for r in range(MOBA_TOPK):
        m = jnp.max(cur, axis=0, keepdims=True)
        idx = jnp.min(jnp.where(cur == m, blk, nb), axis=0, keepdims=True)
        o_ref[r] = jnp.broadcast_to(idx[0], (N_HEADS, LANES))
        cur = jnp.where(blk == idx, -jnp.inf, cur)


def _decode_gate(q, kmean):
    B, nb = kmean.shape[:2]
    out = pl.pallas_call(
        _decode_gate_kernel,
        grid=(B,),
        in_specs=[
            pl.BlockSpec((None, N_HEADS, HEAD_DIM), lambda b: (b, 0, 0)),
            pl.BlockSpec((None, nb, N_HEADS, HEAD_DIM), lambda b: (b, 0, 0, 0)),
        ],
        out_specs=pl.BlockSpec((None, MOBA_TOPK, N_HEADS, LANES), lambda b: (b, 0, 0, 0)),
        out_shape=jax.ShapeDtypeStruct((B, MOBA_TOPK, N_HEADS, LANES), I32),
        compiler_params=_params(("parallel",), 4 * nb * N_HEADS * HEAD_DIM * 4),
        name="moba_decode_gate",
    )(q, kmean)
    return out[..., 0].transpose(0, 2, 1)


def _decode_attend_kernel(pages_ref, blocks_ref, q_ref, kn_ref, vn_ref, tab_ref, *refs, past_len):
    n_src = MOBA_TOPK * 2
    k_refs, v_refs, o_ref = refs[:n_src], refs[n_src:2 * n_src], refs[2 * n_src]
    b, h = pl.program_id(0), pl.program_id(1)
    scale = HEAD_DIM ** -0.5
    q = q_ref[...]
    tab = tab_ref[...]
    row = lax.broadcasted_iota(I32, (PAGE_SIZE, 1), 0)
    lane = lax.broadcasted_iota(I32, (PAGE_SIZE, 2 * LANES), 1)
    logits = []
    for j in range(n_src):
        blk = blocks_ref[(b * N_HEADS + h) * MOBA_TOPK + j // 2]
        rel = past_len - (blk * MOBA_BLOCK + (j % 2) * PAGE_SIZE + row)
        bias = jnp.sum(jnp.where(lane == jnp.minimum(rel, REL_MAX_DIST), tab, 0.0), axis=-1, keepdims=True)
        logits.append(jnp.sum(k_refs[j][...] * q, axis=-1, keepdims=True) * scale + bias)
    own = jnp.sum(kn_ref[...] * q, axis=-1, keepdims=True) * scale + tab[:, 0:1]
    m = own
    for s in logits:
        m = jnp.maximum(m, jnp.max(s, axis=0, keepdims=True))
    p_own = jnp.exp(own - m)
    l = p_own
    acc = p_own * vn_ref[...]
    for j in range(n_src):
        p = jnp.exp(logits[j] - m)
        l = l + jnp.sum(p, axis=0, keepdims=True)
        acc = acc + jnp.sum(p * v_refs[j][...], axis=0, keepdims=True)
    o_ref[...] = acc / l


def _moba_decode(q, k_new, v_new, cache_k, cache_v, layer, page_table, rel_bias):
    B, n_pages = page_table.shape
    past_len = n_pages * PAGE_SIZE
    assert past_len % MOBA_BLOCK == 0 and past_len // MOBA_BLOCK >= MOBA_TOPK
    H = N_HEADS
    kmean = _paged_block_mean(cache_k, layer, page_table)
    blocks = _decode_gate(q, kmean)
    n_layers, n_pool = cache_k.shape[:2]
    pages = layer * n_pool + jnp.take_along_axis(
        page_table[:, None, :], (blocks[..., None] * 2 + jnp.arange(2, dtype=I32)).reshape(B, H, -1), axis=2)
    n_src = MOBA_TOPK * 2
    tab = rel_bias[jnp.asarray(_rel_bucket_table())].T
    tab = jnp.pad(tab, ((0, 0), (0, 2 * LANES - tab.shape[1]))).reshape(H, 1, 2 * LANES)
    ck = cache_k.reshape(n_layers * n_pool, PAGE_SIZE, D_MODEL)
    cv = cache_v.reshape(n_layers * n_pool, PAGE_SIZE, D_MODEL)

    def page_spec(j):
        return pl.BlockSpec((None, PAGE_SIZE, HEAD_DIM),
                            lambda b, h, pg, bk: (pg[(b * H + h) * n_src + j], 0, h))

    vec_spec = pl.BlockSpec((None, None, 1, HEAD_DIM), lambda b, h, pg, bk: (b, h, 0, 0))
    grid_spec = pltpu.PrefetchScalarGridSpec(
        num_scalar_prefetch=2,
        grid=(B, H),
        in_specs=[vec_spec, vec_spec, vec_spec,
                  pl.BlockSpec((None, 1, 2 * LANES), lambda b, h, pg, bk: (h, 0, 0))]
                 + [page_spec(j) for j in range(n_src)] * 2,
        out_specs=vec_spec,
    )
    r4 = lambda a: a.reshape(B, H, 1, HEAD_DIM)
    out = pl.pallas_call(
        functools.partial(_decode_attend_kernel, past_len=past_len),
        grid_spec=grid_spec,
        out_shape=jax.ShapeDtypeStruct((B, H, 1, HEAD_DIM), F32),
        compiler_params=_params(("parallel", "arbitrary"), 4 * n_src * PAGE_SIZE * HEAD_DIM * 4),
        name="moba_decode_attend",
    )(pages.reshape(-1), blocks.reshape(-1), r4(q), r4(k_new), r4(v_new), tab,
      *([ck] * n_src), *([cv] * n_src))
    return out.reshape(B, D_MODEL)


def _peer_select_kernel(qq_ref, k1_ref, k2_ref, a_ref, b_ref, g_ref):
    tq = qq_ref.shape[0]
    k1 = k1_ref[...]
    k2 = k2_ref[...]
    a_rows, b_rows, g_rows = [], [], []
    for h in range(PEER_HEADS):
        q1 = qq_ref[:, (2 * h) * PEER_HALF:(2 * h + 1) * PEER_HALF]
        q2 = qq_ref[:, (2 * h + 1) * PEER_HALF:(2 * h + 2) * PEER_HALF]
        v1, i1 = _topk_rows(_dot_nt_x3(k1, q1), PEER_TOPK)
        v2, i2 = _topk_rows(_dot_nt_x3(k2, q2), PEER_TOPK)
        cand, pos_ids = [], []
        for r in range(PEER_TOPK):
            cnt = PEER_TOPK // (r + 1)
            cand.append(v1[r:r + 1] + v2[:cnt])
            pos_ids.append((lax.broadcasted_iota(I32, (cnt, tq), 0) + r * PEER_TOPK).astype(F32))
        sc, pos = _topk_rows(jnp.concatenate(cand, axis=0), PEER_TOPK, jnp.concatenate(pos_ids, axis=0))
        pos = pos.astype(I32)
        r_k = pos >> 4
        c_k = pos & (PEER_TOPK - 1)
        a_k = jnp.zeros_like(sc)
        b_k = jnp.zeros_like(sc)
        for r in range(PEER_TOPK):
            a_k = jnp.where(r_k == r, i1[r:r + 1], a_k)
            b_k = jnp.where(c_k == r, i2[r:r + 1], b_k)
        e = jnp.exp(sc - sc[0:1])
        g_rows.append(e / jnp.sum(e, axis=0, keepdims=True))
        a_rows.append(a_k)
        b_rows.append(b_k)
    a_ref[...] = jnp.concatenate(a_rows, axis=0).T
    b_ref[...] = jnp.concatenate(b_rows, axis=0).T
    g_ref[...] = jnp.concatenate(g_rows, axis=0).T


def _peer_select(qq, keys1, keys2):
    T = qq.shape[0]
    tq = min(T, 256)
    assert T % tq == 0
    spec = pl.BlockSpec((tq, PEER_PAIRS), lambda i: (i, 0))
    key_spec = pl.BlockSpec((PEER_N_KEYS, PEER_HALF), lambda i: (0, 0))
    shape = jax.ShapeDtypeStruct((T, PEER_PAIRS), F32)
    return pl.pallas_call(
        _peer_select_kernel,
        grid=(T // tq,),
        in_specs=[pl.BlockSpec((tq, 2 * PEER_HEADS * PEER_HALF), lambda i: (i, 0)), key_spec, key_spec],
        out_specs=[spec, spec, spec],
        out_shape=[shape, shape, shape],
        compiler_params=_params(("parallel",), 4 * tq * 2048 * 4),
        name="peer_select",
    )(qq, keys1, keys2)


def _gelu_exact(x):
    return 0.5 * x * (1.0 + lax.erf(x * (2.0 ** -0.5)))


PEER_TOKEN_TILE = 512
PEER_BUILD_ROWS = 16
PEER_KEY1_PER_TILE = 8


def _peer_mix_kernel(a_ref, b_ref, g_ref, h_ref, u_ref, v_ref, x_ref, y_ref, gmat_scr, acc_scr, *, groups):
    j = pl.program_id(1)
    tq = h_ref.shape[0]
    rows = PEER_BUILD_ROWS

    @pl.when(j == 0)
    def _():
        acc_scr[...] = jnp.zeros_like(acc_scr)
        key = lax.broadcasted_iota(I32, (PEER_N_KEYS, PEER_PAIRS), 0).astype(F32)

        def build(tg, carry):
            t0 = pl.multiple_of(tg * rows, rows)
            a8 = a_ref[pl.ds(t0, rows), :]
            b8 = b_ref[pl.ds(t0, rows), :]
            g8 = g_ref[pl.ds(t0, rows), :]
            mats = []
            for r in range(rows):
                a_hot = (key == a8[r:r + 1]).astype(BF16)
                b_hot = jnp.where(key == b8[r:r + 1], g8[r:r + 1], 0.0).astype(BF16)
                mats.append(_dot_nt(a_hot, b_hot).astype(gmat_scr.dtype))
            gmat_scr[:, pl.ds(t0, rows), :] = jnp.swapaxes(jnp.stack(mats, axis=0), 0, 1)
            return carry

        lax.fori_loop(0, tq // rows, build, 0)

    h = h_ref[...]
    total = None
    for c in range(groups // 2):
        rows = slice(c * 2 * PEER_N_KEYS, (c + 1) * 2 * PEER_N_KEYS)
        act = _gelu_exact(_dot_nt(h, u_ref[rows, :]))
        w = jnp.concatenate([gmat_scr[j * groups + 2 * c], gmat_scr[j * groups + 2 * c + 1]], axis=-1)
        z = (act * w.astype(F32)).astype(BF16)
        part = jnp.dot(z, v_ref[rows, :], preferred_element_type=F32)
        total = part if total is None else total + part
    acc_scr[...] += total

    @pl.when(j == pl.num_programs(1) - 1)
    def _():
        y_ref[...] = x_ref[...] + acc_scr[...]


def _peer_mix(a_idx, b_idx, g, h_bf16, u_bf16, v_bf16, x):
    T, D = x.shape
    E = u_bf16.shape[0]
    tq = min(T, PEER_TOKEN_TILE)
    groups = PEER_KEY1_PER_TILE
    te = groups * PEER_N_KEYS
    assert T % tq == 0 and tq % PEER_BUILD_ROWS == 0 and E % te == 0 and E == PEER_N_KEYS * PEER_N_KEYS
    pair_spec = pl.BlockSpec((tq, PEER_PAIRS), lambda i, j: (i, 0))
    tok_spec = pl.BlockSpec((tq, D), lambda i, j: (i, 0))
    est = tq * PEER_N_KEYS * PEER_N_KEYS * 2 + 2 * (2 * te * D * 2 + 3 * tq * D * 4) + 4 * tq * te * 4
    return pl.pallas_call(
        functools.partial(_peer_mix_kernel, groups=groups),
        grid=(T // tq, E // te),
        in_specs=[
            pair_spec, pair_spec, pair_spec, tok_spec,
            pl.BlockSpec((te, D), lambda i, j: (j, 0)),
            pl.BlockSpec((te, D), lambda i, j: (j, 0)),
            tok_spec,
        ],
        out_specs=tok_spec,
        out_shape=jax.ShapeDtypeStruct((T, D), F32),
        scratch_shapes=[
            pltpu.VMEM((PEER_N_KEYS, tq, PEER_N_KEYS), BF16),
            pltpu.VMEM((tq, D), F32),
        ],
        compiler_params=_params(("parallel", "arbitrary"), est),
        name="peer_mix",
    )(a_idx, b_idx, g, h_bf16, u_bf16, v_bf16, x)


def _peer(x, norm_gain, w_q, keys1, keys2, u_bf16, v_bf16):
    qq, h_bf16 = _norm_proj(x, norm_gain, w_q, emit_h=True)
    a_idx, b_idx, g = _peer_select(qq, keys1, keys2)
    return _peer_mix(a_idx, b_idx, g, h_bf16, u_bf16, v_bf16, x)


def kernel(x_prompt, x_sample, state_hgrn, cache_k, cache_v, page_table, norm_mix, norm_ffn, hgrn_w_in, hgrn_lb,
           hgrn_o_norm, hgrn_w_out, moba_w_in, moba_q_norm, moba_k_norm, moba_w_out, rel_bias, peer_w_q,
           peer_keys1, peer_keys2, peer_u, peer_v):
    depth = norm_mix.shape[0]
    Bp, Tp, D = x_prompt.shape
    Bs, Ts, _ = x_sample.shape
    assert Bp == 1 and Ts == 1 and D == D_MODEL
    H = N_HEADS
    lbs = jnp.cumsum(jax.nn.softmax(hgrn_lb.astype(F32), axis=0), axis=0)
    xp = x_prompt.reshape(Tp, D)
    xs = x_sample.reshape(Bs, D)
    st_p, st_s, kp_rows, vp_rows, ks_rows, vs_rows = [], [], [], [], [], []
    for layer in range(depth):
        if layer % 2 == 0:
            a = layer // 2
            w_in = hgrn_w_in[a].astype(BF16)
            w_out = hgrn_w_out[a].astype(BF16)
            proj_p, = _norm_proj(xp, norm_mix[layer], w_in)
            proj_s, = _norm_proj(xs, norm_mix[layer], w_in)
            o_p, s_p = _hgrn_prompt(proj_p, lbs[a])
            o_s, s_s = _hgrn_step(proj_s, lbs[a], state_hgrn[a])
            xp = _out_proj(o_p, w_out, xp, gate_src=proj_p, gate_col=3, gain=hgrn_o_norm[a])
            xs = _out_proj(o_s, w_out, xs, gate_src=proj_s, gate_col=3, gain=hgrn_o_norm[a])
            st_p.append(s_p[None])
            st_s.append(s_s)
        else:
            b = layer // 2
            w_in = moba_w_in[b].astype(BF16)
            w_out = moba_w_out[b].astype(BF16)
            head_gain = jnp.concatenate([jnp.tile(moba_q_norm[b], H), jnp.tile(moba_k_norm[b], H), jnp.ones((D,), F32)])
            qkv_p, qkv_p16, vt_p16 = _norm_proj(xp, norm_mix[layer], w_in, head_gain=head_gain, n_head_norm_tiles=2,
                                                emit_bf16=True, emit_tail_t=True)
            qkv_s, = _norm_proj(xs, norm_mix[layer], w_in, head_gain=head_gain, n_head_norm_tiles=2)
            att_p = _moba_prompt(qkv_p, qkv_p16, vt_p16, rel_bias)
            q_s = qkv_s[:, :D].reshape(Bs, H, HEAD_DIM)
            k_s = qkv_s[:, D:2 * D].reshape(Bs, H, HEAD_DIM)
            v_s = qkv_s[:, 2 * D:].reshape(Bs, H, HEAD_DIM)
            att_s = _moba_decode(q_s, k_s, v_s, cache_k, cache_v, b, page_table, rel_bias)
            xp = _out_proj(att_p, w_out, xp)
            xs = _out_proj(att_s, w_out, xs)
            kp_rows.append(qkv_p[:, D:2 * D].reshape(Bp, Tp, H, HEAD_DIM))
            vp_rows.append(qkv_p[:, 2 * D:].reshape(Bp, Tp, H, HEAD_DIM))
            ks_rows.append(k_s.reshape(Bs, Ts, H, HEAD_DIM))
            vs_rows.append(v_s.reshape(Bs, Ts, H, HEAD_DIM))
        w_q = peer_w_q[layer].astype(BF16)
        u16 = peer_u[layer].astype(BF16)
        v16 = peer_v[layer].astype(BF16)
        xp = _peer(xp, norm_ffn[layer], w_q, peer_keys1[layer], peer_keys2[layer], u16, v16)
        xs = _peer(xs, norm_ffn[layer], w_q, peer_keys1[layer], peer_keys2[layer], u16, v16)
    return (xp.reshape(Bp, Tp, D), xs.reshape(Bs, Ts, D), jnp.stack(st_p), jnp.stack(st_s),
            jnp.stack(kp_rows), jnp.stack(vp_rows), jnp.stack(ks_rows), jnp.stack(vs_rows))
```

```python
import functools
import math

import numpy as np
import jax
import jax.numpy as jnp
from jax import lax
from jax.experimental import pallas as pl
from jax.experimental.pallas import tpu as pltpu

F32 = jnp.float32
BF16 = jnp.bfloat16
I32 = jnp.int32

D_MODEL = 1024
HEAD_DIM = 128
N_HEADS = D_MODEL // HEAD_DIM
PAGE_SIZE = 128
HGRN_CHUNK = 64
MOBA_BLOCK = 256
MOBA_TOPK = 3
REL_BUCKETS = 32
REL_MAX_DIST = 128
PEER_HEADS = 8
PEER_N_KEYS = 128
PEER_HALF = 128
PEER_TOPK = 16
PEER_PAIRS = PEER_HEADS * PEER_TOPK
RMS_EPS = 1e-6

LANES = 128
SUBLANES = 8
V7X_VMEM_BYTES = 64 * 1024 * 1024
NEG_BIG = -1e30
LOG2_E = 1.4426950408889634


def _vmem_limit(estimate_bytes):
    return int(min(max(2 * estimate_bytes, 16 * 1024 * 1024), V7X_VMEM_BYTES - 8 * 1024 * 1024))


def _params(sem, estimate_bytes):
    return pltpu.CompilerParams(dimension_semantics=sem, vmem_limit_bytes=_vmem_limit(estimate_bytes))


def _split2(x):
    hi = x.astype(BF16)
    lo = (x - hi.astype(F32)).astype(BF16)
    return hi, lo


def _split3(x):
    hi = x.astype(BF16)
    r = x - hi.astype(F32)
    mid = r.astype(BF16)
    lo = (r - mid.astype(F32)).astype(BF16)
    return hi, mid, lo


_NT = (((1,), (1,)), ((), ()))
_TN = (((0,), (0,)), ((), ()))


def _dot_nt(a, b):
    return lax.dot_general(a, b, _NT, preferred_element_type=F32)


def _dot_tn(a, b):
    return lax.dot_general(a, b, _TN, preferred_element_type=F32)


def _dot_nt_x3(a, b):
    ah, al = _split2(a)
    bh, bl = _split2(b)
    return _dot_nt(ah, bh) + _dot_nt(ah, bl) + _dot_nt(al, bh)


def _sigmoid(x):
    return 1.0 / (1.0 + jnp.exp(-x))


def _head_rms(y, gain):
    outs = []
    for h in range(y.shape[1] // HEAD_DIM):
        yh = y[:, h * HEAD_DIM:(h + 1) * HEAD_DIM]
        ms = jnp.mean(yh * yh, axis=-1, keepdims=True)
        outs.append(yh * lax.rsqrt(ms + RMS_EPS))
    return jnp.concatenate(outs, axis=-1) * gain


def _topk_rows(s, k, ids=None):
    if ids is None:
        ids = lax.broadcasted_iota(I32, s.shape, 0).astype(F32)
    vals, idxs = [], []
    cur = s
    for _ in range(k):
        m = jnp.max(cur, axis=0, keepdims=True)
        idx = jnp.min(jnp.where(cur == m, ids, jnp.inf), axis=0, keepdims=True)
        vals.append(m)
        idxs.append(idx)
        cur = jnp.where(ids == idx, -jnp.inf, cur)
    return jnp.concatenate(vals, axis=0), jnp.concatenate(idxs, axis=0)


def _norm_proj_kernel(x_ref, g_ref, w_ref, hg_ref, *rest, n_head_norm_tiles, emit_h, emit_bf16, emit_tail_t):
    rest = list(rest)
    wt_ref = rest.pop(0) if emit_tail_t else None
    y_ref = rest.pop(0)
    x = x_ref[...]
    ms = jnp.mean(x * x, axis=-1, keepdims=True)
    h = (x * lax.rsqrt(ms + RMS_EPS) * g_ref[...]).astype(BF16)
    y = jnp.dot(h, w_ref[...], preferred_element_type=F32)
    j = pl.program_id(1)
    if n_head_norm_tiles:
        y = jnp.where(j < n_head_norm_tiles, _head_rms(y, hg_ref[...]), y)
    y_ref[...] = y
    if emit_bf16:
        rest.pop(0)[...] = y.astype(BF16)
    if emit_h:
        rest.pop(0)[...] = h
    if emit_tail_t:
        yt_ref = rest.pop(0)

        @pl.when(j == pl.num_programs(1) - 1)
        def _():
            yt_ref[...] = _dot_nt(wt_ref[...], h).astype(BF16)


def _norm_proj(x, gain, w_bf16, *, head_gain=None, n_head_norm_tiles=0, emit_h=False, emit_bf16=False,
               emit_tail_t=False, tn=1024):
    T, D = x.shape
    N = w_bf16.shape[1]
    tm = min(T, 1024)
    assert T % tm == 0 and N % tn == 0
    n_tiles = N // tn
    if head_gain is None:
        head_gain = jnp.ones((N,), F32)
    operands = [x, gain.reshape(1, D), w_bf16, head_gain.reshape(1, N)]
    in_specs = [
        pl.BlockSpec((tm, D), lambda i, j: (i, 0)),
        pl.BlockSpec((1, D), lambda i, j: (0, 0)),
        pl.BlockSpec((D, tn), lambda i, j: (0, j)),
        pl.BlockSpec((1, tn), lambda i, j: (0, j)),
    ]
    out_shape = [jax.ShapeDtypeStruct((T, N), F32)]
    out_specs = [pl.BlockSpec((tm, tn), lambda i, j: (i, j))]
    if emit_bf16:
        out_shape.append(jax.ShapeDtypeStruct((T, N), BF16))
        out_specs.append(pl.BlockSpec((tm, tn), lambda i, j: (i, j)))
    if emit_h:
        out_shape.append(jax.ShapeDtypeStruct((T, D), BF16))
        out_specs.append(pl.BlockSpec((tm, D), lambda i, j: (i, 0)))
    if emit_tail_t:
        assert n_head_norm_tiles < n_tiles
        operands.append(w_bf16[:, N - tn:].T)
        in_specs.append(pl.BlockSpec((tn, D), lambda i, j: (0, 0)))
        out_shape.append(jax.ShapeDtypeStruct((tn, T), BF16))
        out_specs.append(pl.BlockSpec((tn, tm), lambda i, j: (0, i)))
    est = 2 * (tm * D * 4 + 2 * D * tn * 2 + tm * tn * 8 + tm * D * 2) + tm * tn * 8
    outs = pl.pallas_call(
        functools.partial(_norm_proj_kernel, n_head_norm_tiles=n_head_norm_tiles, emit_h=emit_h, emit_bf16=emit_bf16,
                          emit_tail_t=emit_tail_t),
        grid=(T // tm, n_tiles),
        in_specs=in_specs,
        out_specs=out_specs,
        out_shape=out_shape,
        compiler_params=_params(("parallel", "arbitrary"), est),
        name="norm_proj",
    )(*operands)
    return outs


def _out_proj_kernel(o_ref, gate_ref, gain_ref, w_ref, x_ref, y_ref, *, gated):
    o = o_ref[...]
    if gated:
        g = gate_ref[...]
        o = _head_rms(o, gain_ref[...]) * (g * _sigmoid(g))
    y_ref[...] = x_ref[...] + jnp.dot(o.astype(BF16), w_ref[...], preferred_element_type=F32)


def _out_proj(o, w_bf16, x, *, gate_src=None, gate_col=0, gain=None):
    T, D = o.shape
    tm = min(T, 512)
    assert T % tm == 0
    gated = gate_src is not None
    if not gated:
        gate_src, gain = o, jnp.ones((D,), F32)
    est = 2 * (4 * tm * D * 4 + D * D * 2)
    return pl.pallas_call(
        functools.partial(_out_proj_kernel, gated=gated),
        grid=(T // tm,),
        in_specs=[
            pl.BlockSpec((tm, D), lambda i: (i, 0)),
            pl.BlockSpec((tm, D), lambda i: (i, gate_col)),
            pl.BlockSpec((1, D), lambda i: (0, 0)),
            pl.BlockSpec((D, D), lambda i: (0, 0)),
            pl.BlockSpec((tm, D), lambda i: (i, 0)),
        ],
        out_specs=pl.BlockSpec((tm, D), lambda i: (i, 0)),
        out_shape=jax.ShapeDtypeStruct((T, D), F32),
        compiler_params=_params(("parallel",), est),
        name="out_proj",
    )(o, gate_src, gain.reshape(1, D), w_bf16, x)


HGRN_SAFE_DECAY = 80.0


def _hgrn_prompt_kernel(q_ref, fl_ref, v_ref, lb_ref, o_ref, s_out_ref, st_scr, b_scr, k_scr):
    c = pl.program_id(0)
    C = q_ref.shape[0]

    @pl.when(c == 0)
    def _():
        st_scr[...] = jnp.zeros_like(st_scr)

    lb = lb_ref[...]
    f = lb + (1.0 - lb) * _sigmoid(fl_ref[...])
    logf = jnp.log(f)
    q = q_ref[...]
    qq = q * _sigmoid(q)
    tri = (lax.broadcasted_iota(I32, (C, C), 0) >= lax.broadcasted_iota(I32, (C, C), 1)).astype(BF16)
    p0, p1, p2 = _split3(logf)
    b = (jnp.dot(tri, p0, preferred_element_type=F32) + jnp.dot(tri, p1, preferred_element_type=F32)
         + jnp.dot(tri, p2, preferred_element_type=F32))
    b_scr[...] = b
    k_scr[...] = 1.0 - f
    row = lax.broadcasted_iota(I32, (C, HEAD_DIM), 0)
    col = lax.broadcasted_iota(I32, (C, C), 1)
    causal = lax.broadcasted_iota(I32, (C, C), 0) >= col

    def head_step(h, hs, a_mat):
        bh = b_scr[:, hs]
        vh = v_ref[:, hs].astype(BF16)
        bl = bh[C - 1:C, :]
        qh = q_ref[:, hs]
        qh = qh * _sigmoid(qh)
        st = st_scr[h]
        inter = _dot_nt((qh * jnp.exp(bh)).astype(BF16), st.astype(BF16))
        intra = jnp.dot(a_mat.astype(BF16), vh, preferred_element_type=F32)
        o_ref[:, hs] = inter + intra
        k_dec = (k_scr[:, hs] * jnp.exp(bl - bh)).astype(BF16)
        st_scr[h] = st * jnp.exp(bl) + _dot_tn(vh, k_dec)

    safe = jnp.min(b[C - 1:C, :]) >= -HGRN_SAFE_DECAY

    @pl.when(safe)
    def _():
        for h in range(N_HEADS):
            hs = slice(h * HEAD_DIM, (h + 1) * HEAD_DIM)
            bh = b[:, hs]
            half = 0.5 * bh[C - 1:C, :]
            q_dec = (qq[:, hs] * jnp.exp(bh - half)).astype(BF16)
            k_inc = (k_scr[:, hs] * jnp.exp(half - bh)).astype(BF16)
            head_step(h, hs, jnp.where(causal, _dot_nt(q_dec, k_inc), 0.0))

    @pl.when(jnp.logical_not(safe))
    def _():
        for h in range(N_HEADS):
            hs = slice(h * HEAD_DIM, (h + 1) * HEAD_DIM)
            bh = b[:, hs]
            qh = qq[:, hs]

            def pair_cols(s8, a_mat):
                start = pl.multiple_of(s8 * SUBLANES, SUBLANES)
                b8 = b_scr[pl.ds(start, SUBLANES), hs]
                k8 = k_scr[pl.ds(start, SUBLANES), hs]
                for r in range(SUBLANES):
                    s = start + r
                    d = jnp.exp(jnp.where(row >= s, bh - b8[r:r + 1], -jnp.inf))
                    a_col = jnp.sum(qh * k8[r:r + 1] * d, axis=-1, keepdims=True)
                    a_mat = jnp.where(col == s, a_col, a_mat)
                return a_mat

            head_step(h, hs, lax.fori_loop(0, C // SUBLANES, pair_cols, jnp.zeros((C, C), F32)))

    @pl.when(c == pl.num_programs(0) - 1)
    def _():
        for h in range(N_HEADS):
            s_out_ref[h] = st_scr[h].T


def _hgrn_prompt(proj, lb):
    T = proj.shape[0]
    C = min(HGRN_CHUNK, T)
    assert T % C == 0 and C % SUBLANES == 0
    D = D_MODEL
    est = 2 * 4 * C * D * 4 + 3 * N_HEADS * HEAD_DIM * HEAD_DIM * 4 + 2 * C * D * 4
    return pl.pallas_call(
        _hgrn_prompt_kernel,
        grid=(T // C,),
        in_specs=[
            pl.BlockSpec((C, D), lambda c: (c, 0)),
            pl.BlockSpec((C, D), lambda c: (c, 1)),
            pl.BlockSpec((C, D), lambda c: (c, 2)),
            pl.BlockSpec((1, D), lambda c: (0, 0)),
        ],
        out_specs=[
            pl.BlockSpec((C, D), lambda c: (c, 0)),
            pl.BlockSpec((N_HEADS, HEAD_DIM, HEAD_DIM), lambda c: (0, 0, 0)),
        ],
        out_shape=[
            jax.ShapeDtypeStruct((T, D), F32),
            jax.ShapeDtypeStruct((N_HEADS, HEAD_DIM, HEAD_DIM), F32),
        ],
        scratch_shapes=[
            pltpu.VMEM((N_HEADS, HEAD_DIM, HEAD_DIM), F32),
            pltpu.VMEM((C, D), F32),
            pltpu.VMEM((C, D), F32),
        ],
        compiler_params=_params(("arbitrary",), est),
        name="hgrn_prompt",
    )(proj, proj, proj, lb.reshape(1, D))


def _hgrn_step_kernel(q_ref, fl_ref, lb_ref, v_ref, s_ref, o_ref, s_out_ref):
    lb = lb_ref[...]
    f = lb + (1.0 - lb) * _sigmoid(fl_ref[...])
    q = q_ref[...]
    qq = q * _sigmoid(q)
    s_new = s_ref[...] * f + (1.0 - f) * v_ref[...]
    s_out_ref[...] = s_new
    o_ref[...] = jnp.sum(s_new * qq, axis=1, keepdims=True)


def _hgrn_step(proj, lb, state):
    B = proj.shape[0]
    H, DK = N_HEADS, HEAD_DIM
    q = proj[:, :D_MODEL].reshape(B, H, DK, 1)
    fl = proj[:, D_MODEL:2 * D_MODEL].reshape(B, H, DK, 1)
    v = proj[:, 2 * D_MODEL:3 * D_MODEL].reshape(B, H, 1, DK)
    col_spec = pl.BlockSpec((None, H, DK, 1), lambda b: (b, 0, 0, 0))
    row_spec = pl.BlockSpec((None, H, 1, DK), lambda b: (b, 0, 0, 0))
    est = 2 * (3 * H * DK * LANES * 4 + 2 * H * DK * DK * 4)
    o, s_new = pl.pallas_call(
        _hgrn_step_kernel,
        grid=(B,),
        in_specs=[
            col_spec, col_spec,
            pl.BlockSpec((H, DK, 1), lambda b: (0, 0, 0)),
            row_spec,
            pl.BlockSpec((None, H, DK, DK), lambda b: (b, 0, 0, 0)),
        ],
        out_specs=[row_spec, pl.BlockSpec((None, H, DK, DK), lambda b: (b, 0, 0, 0))],
        out_shape=[jax.ShapeDtypeStruct((B, H, 1, DK), F32), jax.ShapeDtypeStruct(state.shape, F32)],
        compiler_params=_params(("parallel",), est),
        name="hgrn_step",
    )(q, fl, lb.reshape(H, DK, 1), v, state)
    return o.reshape(B, D_MODEL), s_new


def _rel_bucket_table():
    d = np.arange(REL_MAX_DIST + 1)
    max_exact = REL_BUCKETS // 2
    large = max_exact + (np.log(np.maximum(d, max_exact).astype(np.float32) / max_exact)
                         / math.log(REL_MAX_DIST / max_exact) * (REL_BUCKETS - max_exact)).astype(np.int32)
    large = np.minimum(large, REL_BUCKETS - 1)
    tab = np.where(d < max_exact, d, large).astype(np.int32)
    assert tab[REL_MAX_DIST] == REL_BUCKETS - 1
    return tab


def _block_mean_kernel(k_ref, o_ref):
    o_ref[...] = jnp.mean(k_ref[...], axis=0, keepdims=True)


def _block_mean(k_src, col_block, n_blocks):
    return pl.pallas_call(
        _block_mean_kernel,
        grid=(n_blocks,),
        in_specs=[pl.BlockSpec((MOBA_BLOCK, D_MODEL), lambda n: (n, col_block))],
        out_specs=pl.BlockSpec((None, 1, D_MODEL), lambda n: (n, 0, 0)),
        out_shape=jax.ShapeDtypeStruct((n_blocks, 1, D_MODEL), F32),
        compiler_params=_params(("parallel",), 4 * MOBA_BLOCK * D_MODEL * 4),
        name="moba_block_mean",
    )(k_src)


MOBA_FAR_GROUP = 4


def _moba_prompt_kernel(q_ref, kmean_ref, k_ref, vt_ref, bias_ref, far_ref, o_ref, sel_scr, m_scr, l_scr, acc_scr,
                        s_scr, *, group):
    i = pl.program_id(1)
    tq = q_ref.shape[0]
    nb = kmean_ref.shape[0]
    q = q_ref[...]
    qs = (q * (HEAD_DIM ** -0.5 * LOG2_E)).astype(BF16)

    gate = _dot_nt_x3(kmean_ref[...], q)
    blk = lax.broadcasted_iota(I32, (nb, tq), 0)
    cur = jnp.where(blk < i, gate, -jnp.inf)
    sel = jnp.zeros((nb, tq), F32)
    for _ in range(MOBA_TOPK):
        m = jnp.max(cur, axis=0, keepdims=True)
        idx = jnp.min(jnp.where((cur == m) & (blk < i), blk, nb), axis=0, keepdims=True)
        hit = (blk == idx) & (cur > -jnp.inf)
        sel = jnp.where(hit, 1.0, sel)
        cur = jnp.where(blk == idx, -jnp.inf, cur)
    pad = sel_scr.shape[0] - nb
    sel_scr[...] = jnp.concatenate([sel, jnp.zeros((pad, tq), F32)], axis=0) if pad else sel
    sub = lax.broadcasted_iota(I32, (SUBLANES, tq), 0)

    def allowed(n):
        start = pl.multiple_of((n // SUBLANES) * SUBLANES, SUBLANES)
        rows = sel_scr[pl.ds(start, SUBLANES), :]
        return jnp.max(jnp.where(sub == n - start, rows, 0.0), axis=0, keepdims=True) > 0.0

    m_scr[...] = jnp.full_like(m_scr, NEG_BIG)
    l_scr[...] = jnp.zeros_like(l_scr)
    acc_scr[...] = jnp.zeros_like(acc_scr)

    def attend(n, bias, allow):
        start = pl.multiple_of(n * MOBA_BLOCK, MOBA_BLOCK)
        s = _dot_nt(k_ref[pl.ds(start, MOBA_BLOCK), :], qs) + bias
        m_old = m_scr[...]
        m_new = jnp.maximum(m_old, jnp.max(jnp.where(allow, s, NEG_BIG), axis=0, keepdims=True))
        alpha = jnp.exp2(m_old - m_new)
        p = jnp.where(allow, jnp.exp2(s - m_new), 0.0)
        l_scr[...] = alpha * l_scr[...] + jnp.sum(p, axis=0, keepdims=True)
        pv = jnp.dot(vt_ref[:, pl.ds(start, MOBA_BLOCK)], p.astype(BF16), preferred_element_type=F32)
        acc_scr[...] = alpha * acc_scr[...] + pv
        m_scr[...] = m_new

    far = far_ref[0:1, 0:1]

    n_groups_total = nb // group

    def group_start(ng):
        return pl.multiple_of(jnp.minimum(ng, n_groups_total - 1) * (group * MOBA_BLOCK), group * MOBA_BLOCK)

    def far_scores(ng, slot):
        s_scr[slot] = _dot_nt(k_ref[pl.ds(group_start(ng), group * MOBA_BLOCK), :], qs)

    def far_update(ng, slot):
        n0 = ng * group
        start = group_start(ng)
        parts = [s_scr[slot, g * MOBA_BLOCK:(g + 1) * MOBA_BLOCK, :] for g in range(group)]
        allow = [allowed(jnp.minimum(n0 + g, nb - 1)) & (n0 + g < i - 1) for g in range(group)]
        mx = jnp.where(allow[0], jnp.max(parts[0], axis=0, keepdims=True), NEG_BIG)
        for g in range(1, group):
            mx = jnp.maximum(mx, jnp.where(allow[g], jnp.max(parts[g], axis=0, keepdims=True), NEG_BIG))
        m_old = m_scr[...]
        m_new = jnp.maximum(m_old, mx + far)
        alpha = jnp.exp2(m_old - m_new)
        shift = m_new - far
        l_new = alpha * l_scr[...]
        acc = alpha * acc_scr[...]
        for g in range(group):
            p = jnp.exp2(parts[g] - shift)
            l_new = l_new + jnp.where(allow[g], jnp.sum(p, axis=0, keepdims=True), 0.0)
            start_g = pl.multiple_of(start + g * MOBA_BLOCK, MOBA_BLOCK)
            pv = jnp.dot(vt_ref[:, pl.ds(start_g, MOBA_BLOCK)], p.astype(BF16), preferred_element_type=F32)
            acc = acc + jnp.where(allow[g], pv, 0.0)
        l_scr[...] = l_new
        acc_scr[...] = acc
        m_scr[...] = m_new

    n_groups = (i + group - 2) // group
    far_scores(0, 0)

    def far_pair(p, carry):
        far_scores(2 * p + 1, 1)
        far_update(2 * p, 0)
        far_scores(2 * p + 2, 0)
        far_update(2 * p + 1, 1)
        return carry

    lax.fori_loop(0, (n_groups + 1) // 2, far_pair, 0)

    @pl.when(i >= 1)
    def _():
        attend(i - 1, bias_ref[1], allowed(i - 1))

    causal = lax.broadcasted_iota(I32, (MOBA_BLOCK, tq), 0) <= lax.broadcasted_iota(I32, (MOBA_BLOCK, tq), 1)
    attend(i, bias_ref[0], causal)
    o_ref[...] = (acc_scr[...] / l_scr[...]).T


def _moba_prompt(qkv, qkv_bf16, vt_bf16, rel_bias):
    T = qkv.shape[0]
    assert T % MOBA_BLOCK == 0
    nb = T // MOBA_BLOCK
    H = N_HEADS
    kmean = _block_mean(qkv, 1, nb).reshape(nb, D_MODEL)
    tab = rel_bias[jnp.asarray(_rel_bucket_table())] * LOG2_E
    n = MOBA_BLOCK
    e = np.arange(2 * n)
    e_signed = np.where(e < n, e, e - 2 * n)
    w_own = tab[np.clip(e_signed, 0, REL_MAX_DIST)]
    w_prev = tab[np.clip(n + e_signed, 0, REL_MAX_DIST)]

    def toeplitz(w):
        return jnp.tile(w, n)[:n * (2 * n - 1)].reshape(n, 2 * n - 1)[:, :n]

    near = jnp.stack([jax.vmap(toeplitz, in_axes=1)(w_own), jax.vmap(toeplitz, in_axes=1)(w_prev)], axis=1)
    far = jnp.broadcast_to(tab[REL_MAX_DIST][:, None, None], (H, 8, LANES))
    group = math.gcd(nb, MOBA_FAR_GROUP)
    est = 2 * (2 * T * HEAD_DIM * 2 + 2 * MOBA_BLOCK * MOBA_BLOCK * 4) + (16 + 4 * group) * MOBA_BLOCK * MOBA_BLOCK * 4
    return pl.pallas_call(
        functools.partial(_moba_prompt_kernel, group=group),
        grid=(H, nb),
        in_specs=[
            pl.BlockSpec((MOBA_BLOCK, HEAD_DIM), lambda h, i: (i, h)),
            pl.BlockSpec((nb, HEAD_DIM), lambda h, i: (0, h)),
            pl.BlockSpec((T, HEAD_DIM), lambda h, i: (0, H + h)),
            pl.BlockSpec((HEAD_DIM, T), lambda h, i: (h, 0)),
            pl.BlockSpec((None, 2, MOBA_BLOCK, MOBA_BLOCK), lambda h, i: (h, 0, 0, 0)),
            pl.BlockSpec((None, 8, LANES), lambda h, i: (h, 0, 0)),
        ],
        out_specs=pl.BlockSpec((MOBA_BLOCK, HEAD_DIM), lambda h, i: (i, h)),
        out_shape=jax.ShapeDtypeStruct((T, D_MODEL), F32),
        scratch_shapes=[
            pltpu.VMEM((-(-nb // SUBLANES) * SUBLANES, MOBA_BLOCK), F32),
            pltpu.VMEM((1, MOBA_BLOCK), F32),
            pltpu.VMEM((1, MOBA_BLOCK), F32),
            pltpu.VMEM((HEAD_DIM, MOBA_BLOCK), F32),
            pltpu.VMEM((2, group * MOBA_BLOCK, MOBA_BLOCK), F32),
        ],
        compiler_params=_params(("parallel", "arbitrary"), est),
        name="moba_prompt",
    )(qkv, kmean, qkv_bf16, vt_bf16, near, far)


PAGES_PER_BLOCK = MOBA_BLOCK // PAGE_SIZE
MEAN_BLOCKS_PER_STEP = 4


def _paged_block_mean_kernel(pt_ref, *refs):
    page_refs, o_ref = refs[:-1], refs[-1]
    for n in range(len(page_refs) // PAGES_PER_BLOCK):
        tot = jnp.sum(page_refs[n * PAGES_PER_BLOCK][...], axis=0)
        for p in range(1, PAGES_PER_BLOCK):
            tot = tot + jnp.sum(page_refs[n * PAGES_PER_BLOCK + p][...], axis=0)
        o_ref[n] = tot * (1.0 / MOBA_BLOCK)


def _paged_block_mean(cache_k, layer, page_table):
    B, n_pages = page_table.shape
    assert n_pages % PAGES_PER_BLOCK == 0
    nb = n_pages // PAGES_PER_BLOCK
    bps = math.gcd(nb, MEAN_BLOCKS_PER_STEP)
    pps = bps * PAGES_PER_BLOCK
    page_shape = (None, None, PAGE_SIZE, N_HEADS, HEAD_DIM)

    def page_spec(p):
        return pl.BlockSpec(page_shape, lambda b, n, pt: (layer, pt[b * n_pages + n * pps + p], 0, 0, 0))

    grid_spec = pltpu.PrefetchScalarGridSpec(
        num_scalar_prefetch=1,
        grid=(B, nb // bps),
        in_specs=[page_spec(p) for p in range(pps)],
        out_specs=pl.BlockSpec((None, bps, N_HEADS, HEAD_DIM), lambda b, n, pt: (b, n, 0, 0)),
    )
    return pl.pallas_call(
        _paged_block_mean_kernel,
        grid_spec=grid_spec,
        out_shape=jax.ShapeDtypeStruct((B, nb, N_HEADS, HEAD_DIM), F32),
        compiler_params=_params(("parallel", "arbitrary"), 2 * pps * PAGE_SIZE * D_MODEL * 4),
        name="moba_paged_block_mean",
    )(page_table.reshape(-1), *([cache_k] * pps))


def _decode_gate_kernel(q_ref, kmean_ref, o_ref):
    nb = kmean_ref.shape[0]
    gate = jnp.sum(kmean_ref[...] * q_ref[...][None], axis=-1, keepdims=True)
    blk = lax.broadcasted_iota(I32, gate.shape, 0)
    cur = gate
    for r in range(MOBA_TOPK):
        m = jnp.max(cur, axis=0, keepdims=True)
        idx = jnp.min(jnp.where(cur == m, blk, nb), axis=0, keepdims=True)
        o_ref[r] = jnp.broadcast_to(idx[0], (N_HEADS, LANES))
        cur = jnp.where(blk == idx, -jnp.inf, cur)


def _decode_gate(q, kmean):
    B, nb = kmean.shape[:2]
    out = pl.pallas_call(
        _decode_gate_kernel,
        grid=(B,),
        in_specs=[
            pl.BlockSpec((None, N_HEADS, HEAD_DIM), lambda b: (b, 0, 0)),
            pl.BlockSpec((None, nb, N_HEADS, HEAD_DIM), lambda b: (b, 0, 0, 0)),
        ],
        out_specs=pl.BlockSpec((None, MOBA_TOPK, N_HEADS, LANES), lambda b: (b, 0, 0, 0)),
        out_shape=jax.ShapeDtypeStruct((B, MOBA_TOPK, N_HEADS, LANES), I32),
        compiler_params=_params(("parallel",), 4 * nb * N_HEADS * HEAD_DIM * 4),
        name="moba_decode_gate",
    )(q, kmean)
    return out[..., 0].transpose(0, 2, 1)


def _decode_attend_kernel(pages_ref, blocks_ref, q_ref, kn_ref, vn_ref, tab_ref, *refs, past_len):
    n_src = MOBA_TOPK * 2
    k_refs, v_refs, o_ref = refs[:n_src], refs[n_src:2 * n_src], refs[2 * n_src]
    b, h = pl.program_id(0), pl.program_id(1)
    head_rows = pl.ds(h, PAGE_SIZE, stride=N_HEADS)
    scale = HEAD_DIM ** -0.5
    q = q_ref[...]
    tab = tab_ref[...]
    row = lax.broadcasted_iota(I32, (PAGE_SIZE, 1), 0)
    lane = lax.broadcasted_iota(I32, (PAGE_SIZE, 2 * LANES), 1)
    logits = []
    for j in range(n_src):
        blk = blocks_ref[(b * N_HEADS + h) * MOBA_TOPK + j // 2]
        rel = past_len - (blk * MOBA_BLOCK + (j % 2) * PAGE_SIZE + row)
        bias = jnp.sum(jnp.where(lane == jnp.minimum(rel, REL_MAX_DIST), tab, 0.0), axis=-1, keepdims=True)
        logits.append(jnp.sum(k_refs[j][head_rows, :] * q, axis=-1, keepdims=True) * scale + bias)
    own = jnp.sum(kn_ref[...] * q, axis=-1, keepdims=True) * scale + tab[:, 0:1]
    m = own
    for s in logits:
        m = jnp.maximum(m, jnp.max(s, axis=0, keepdims=True))
    p_own = jnp.exp(own - m)
    l = p_own
    acc = p_own * vn_ref[...]
    for j in range(n_src):
        p = jnp.exp(logits[j] - m)
        l = l + jnp.sum(p, axis=0, keepdims=True)
        acc = acc + jnp.sum(p * v_refs[j][head_rows, :], axis=0, keepdims=True)
    o_ref[...] = acc / l


def _moba_decode(q, k_new, v_new, cache_k, cache_v, layer, page_table, rel_bias):
    B, n_pages = page_table.shape
    past_len = n_pages * PAGE_SIZE
    assert past_len % MOBA_BLOCK == 0 and past_len // MOBA_BLOCK >= MOBA_TOPK
    H = N_HEADS
    kmean = _paged_block_mean(cache_k, layer, page_table)
    blocks = _decode_gate(q, kmean)
    pages = jnp.take_along_axis(
        page_table[:, None, :], (blocks[..., None] * 2 + jnp.arange(2, dtype=I32)).reshape(B, H, -1), axis=2)
    n_src = MOBA_TOPK * 2
    tab = rel_bias[jnp.asarray(_rel_bucket_table())].T
    tab = jnp.pad(tab, ((0, 0), (0, 2 * LANES - tab.shape[1]))).reshape(H, 1, 2 * LANES)
    n_layers, n_pool = cache_k.shape[:2]
    ck = cache_k.reshape(n_layers, n_pool, PAGE_SIZE * H, HEAD_DIM)
    cv = cache_v.reshape(n_layers, n_pool, PAGE_SIZE * H, HEAD_DIM)

    def page_spec(j):
        return pl.BlockSpec((None, None, PAGE_SIZE * H, HEAD_DIM),
                            lambda b, h, pg, bk: (layer, pg[(b * H + h) * n_src + j], 0, 0))

    vec_spec = pl.BlockSpec((None, None, 1, HEAD_DIM), lambda b, h, pg, bk: (b, h, 0, 0))
    grid_spec = pltpu.PrefetchScalarGridSpec(
        num_scalar_prefetch=2,
        grid=(B, H),
        in_specs=[vec_spec, vec_spec, vec_spec,
                  pl.BlockSpec((None, 1, 2 * LANES), lambda b, h, pg, bk: (h, 0, 0))]
                 + [page_spec(j) for j in range(n_src)] * 2,
        out_specs=vec_spec,
    )
    r4 = lambda a: a.reshape(B, H, 1, HEAD_DIM)
    out = pl.pallas_call(
        functools.partial(_decode_attend_kernel, past_len=past_len),
        grid_spec=grid_spec,
        out_shape=jax.ShapeDtypeStruct((B, H, 1, HEAD_DIM), F32),
        compiler_params=_params(("parallel", "arbitrary"), 4 * n_src * PAGE_SIZE * HEAD_DIM * 4),
        name="moba_decode_attend",
    )(pages.reshape(-1), blocks.reshape(-1), r4(q), r4(k_new), r4(v_new), tab,
      *([ck] * n_src), *([cv] * n_src))
    return out.reshape(B, D_MODEL)


def _peer_select_kernel(qq_ref, k1_ref, k2_ref, a_ref, b_ref, g_ref):
    tq = qq_ref.shape[0]
    k1 = k1_ref[...]
    k2 = k2_ref[...]
    a_rows, b_rows, g_rows = [], [], []
    for h in range(PEER_HEADS):
        q1 = qq_ref[:, (2 * h) * PEER_HALF:(2 * h + 1) * PEER_HALF]
        q2 = qq_ref[:, (2 * h + 1) * PEER_HALF:(2 * h + 2) * PEER_HALF]
        v1, i1 = _topk_rows(_dot_nt_x3(k1, q1), PEER_TOPK)
        v2, i2 = _topk_rows(_dot_nt_x3(k2, q2), PEER_TOPK)
        cand, pos_ids = [], []
        for r in range(PEER_TOPK):
            cnt = PEER_TOPK // (r + 1)
            cand.append(v1[r:r + 1] + v2[:cnt])
            pos_ids.append((lax.broadcasted_iota(I32, (cnt, tq), 0) + r * PEER_TOPK).astype(F32))
        sc, pos = _topk_rows(jnp.concatenate(cand, axis=0), PEER_TOPK, jnp.concatenate(pos_ids, axis=0))
        pos = pos.astype(I32)
        r_k = pos >> 4
        c_k = pos & (PEER_TOPK - 1)
        a_k = jnp.zeros_like(sc)
        b_k = jnp.zeros_like(sc)
        for r in range(PEER_TOPK):
            a_k = jnp.where(r_k == r, i1[r:r + 1], a_k)
            b_k = jnp.where(c_k == r, i2[r:r + 1], b_k)
        e = jnp.exp(sc - sc[0:1])
        g_rows.append(e / jnp.sum(e, axis=0, keepdims=True))
        a_rows.append(a_k)
        b_rows.append(b_k)
    a_ref[...] = jnp.concatenate(a_rows, axis=0).T
    b_ref[...] = jnp.concatenate(b_rows, axis=0).T
    g_ref[...] = jnp.concatenate(g_rows, axis=0).T


def _peer_select(qq, keys1, keys2):
    T = qq.shape[0]
    tq = min(T, 256)
    assert T % tq == 0
    spec = pl.BlockSpec((tq, PEER_PAIRS), lambda i: (i, 0))
    key_spec = pl.BlockSpec((PEER_N_KEYS, PEER_HALF), lambda i: (0, 0))
    shape = jax.ShapeDtypeStruct((T, PEER_PAIRS), F32)
    return pl.pallas_call(
        _peer_select_kernel,
        grid=(T // tq,),
        in_specs=[pl.BlockSpec((tq, 2 * PEER_HEADS * PEER_HALF), lambda i: (i, 0)), key_spec, key_spec],
        out_specs=[spec, spec, spec],
        out_shape=[shape, shape, shape],
        compiler_params=_params(("parallel",), 4 * tq * 2048 * 4),
        name="peer_select",
    )(qq, keys1, keys2)


def _gelu_exact(x):
    return 0.5 * x * (1.0 + lax.erf(x * (2.0 ** -0.5)))


PEER_TOKEN_TILE = 512
PEER_BUILD_ROWS = 16
PEER_KEY1_PER_TILE = 8


def _peer_mix_kernel(a_ref, b_ref, g_ref, h_ref, u_ref, v_ref, x_ref, y_ref, gmat_scr, acc_scr, *, groups):
    j = pl.program_id(1)
    tq = h_ref.shape[0]
    rows = PEER_BUILD_ROWS

    @pl.when(j == 0)
    def _():
        acc_scr[...] = jnp.zeros_like(acc_scr)
        key = lax.broadcasted_iota(I32, (PEER_N_KEYS, PEER_PAIRS), 0).astype(F32)

        def build(tg, carry):
            t0 = pl.multiple_of(tg * rows, rows)
            a8 = a_ref[pl.ds(t0, rows), :]
            b8 = b_ref[pl.ds(t0, rows), :]
            g8 = g_ref[pl.ds(t0, rows), :]
            mats = []
            for r in range(rows):
                a_hot = (key == a8[r:r + 1]).astype(BF16)
                b_hot = jnp.where(key == b8[r:r + 1], g8[r:r + 1], 0.0).astype(BF16)
                mats.append(_dot_nt(a_hot, b_hot).astype(gmat_scr.dtype))
            gmat_scr[:, pl.ds(t0, rows), :] = jnp.swapaxes(jnp.stack(mats, axis=0), 0, 1)
            return carry

        lax.fori_loop(0, tq // rows, build, 0)

    act = _gelu_exact(_dot_nt(h_ref[...], u_ref[...]))
    w = jnp.concatenate([gmat_scr[j * groups + a] for a in range(groups)], axis=-1)
    z = (act * w.astype(F32)).astype(BF16)
    acc_scr[...] += jnp.dot(z, v_ref[...], preferred_element_type=F32)

    @pl.when(j == pl.num_programs(1) - 1)
    def _():
        y_ref[...] = x_ref[...] + acc_scr[...]


def _peer_mix(a_idx, b_idx, g, h_bf16, u_bf16, v_bf16, x):
    T, D = x.shape
    E = u_bf16.shape[0]
    tq = min(T, PEER_TOKEN_TILE)
    groups = PEER_KEY1_PER_TILE
    te = groups * PEER_N_KEYS
    assert T % tq == 0 and tq % PEER_BUILD_ROWS == 0 and E % te == 0 and E == PEER_N_KEYS * PEER_N_KEYS
    pair_spec = pl.BlockSpec((tq, PEER_PAIRS), lambda i, j: (i, 0))
    tok_spec = pl.BlockSpec((tq, D), lambda i, j: (i, 0))
    est = tq * PEER_N_KEYS * PEER_N_KEYS * 2 + 2 * (2 * te * D * 2 + 3 * tq * D * 4) + 4 * tq * te * 4
    return pl.pallas_call(
        functools.partial(_peer_mix_kernel, groups=groups),
        grid=(T // tq, E // te),
        in_specs=[
            pair_spec, pair_spec, pair_spec, tok_spec,
            pl.BlockSpec((te, D), lambda i, j: (j, 0)),
            pl.BlockSpec((te, D), lambda i, j: (j, 0)),
            tok_spec,
        ],
        out_specs=tok_spec,
        out_shape=jax.ShapeDtypeStruct((T, D), F32),
        scratch_shapes=[
            pltpu.VMEM((PEER_N_KEYS, tq, PEER_N_KEYS), BF16),
            pltpu.VMEM((tq, D), F32),
        ],
        compiler_params=_params(("parallel", "arbitrary"), est),
        name="peer_mix",
    )(a_idx, b_idx, g, h_bf16, u_bf16, v_bf16, x)


def _peer(x, norm_gain, w_q, keys1, keys2, u_bf16, v_bf16):
    qq, h_bf16 = _norm_proj(x, norm_gain, w_q, emit_h=True)
    a_idx, b_idx, g = _peer_select(qq, keys1, keys2)
    return _peer_mix(a_idx, b_idx, g, h_bf16, u_bf16, v_bf16, x)


def kernel(x_prompt, x_sample, state_hgrn, cache_k, cache_v, page_table, norm_mix, norm_ffn, hgrn_w_in, hgrn_lb,
           hgrn_o_norm, hgrn_w_out, moba_w_in, moba_q_norm, moba_k_norm, moba_w_out, rel_bias, peer_w_q,
           peer_keys1, peer_keys2, peer_u, peer_v):
    depth = norm_mix.shape[0]
    Bp, Tp, D = x_prompt.shape
    Bs, Ts, _ = x_sample.shape
    assert Bp == 1 and Ts == 1 and D == D_MODEL
    H = N_HEADS
    lbs = jnp.cumsum(jax.nn.softmax(hgrn_lb.astype(F32), axis=0), axis=0)
    xp = x_prompt.reshape(Tp, D)
    xs = x_sample.reshape(Bs, D)
    st_p, st_s, kp_rows, vp_rows, ks_rows, vs_rows = [], [], [], [], [], []
    for layer in range(depth):
        if layer % 2 == 0:
            a = layer // 2
            w_in = hgrn_w_in[a].astype(BF16)
            w_out = hgrn_w_out[a].astype(BF16)
            proj_p, = _norm_proj(xp, norm_mix[layer], w_in)
            proj_s, = _norm_proj(xs, norm_mix[layer], w_in)
            o_p, s_p = _hgrn_prompt(proj_p, lbs[a])
            o_s, s_s = _hgrn_step(proj_s, lbs[a], state_hgrn[a])
            xp = _out_proj(o_p, w_out, xp, gate_src=proj_p, gate_col=3, gain=hgrn_o_norm[a])
            xs = _out_proj(o_s, w_out, xs, gate_src=proj_s, gate_col=3, gain=hgrn_o_norm[a])
            st_p.append(s_p[None])
            st_s.append(s_s)
        else:
            b = layer // 2
            w_in = moba_w_in[b].astype(BF16)
            w_out = moba_w_out[b].astype(BF16)
            head_gain = jnp.concatenate([jnp.tile(moba_q_norm[b], H), jnp.tile(moba_k_norm[b], H), jnp.ones((D,), F32)])
            qkv_p, qkv_p16, vt_p16 = _norm_proj(xp, norm_mix[layer], w_in, head_gain=head_gain, n_head_norm_tiles=2,
                                                emit_bf16=True, emit_tail_t=True)
            qkv_s, = _norm_proj(xs, norm_mix[layer], w_in, head_gain=head_gain, n_head_norm_tiles=2)
            att_p = _moba_prompt(qkv_p, qkv_p16, vt_p16, rel_bias)
            q_s = qkv_s[:, :D].reshape(Bs, H, HEAD_DIM)
            k_s = qkv_s[:, D:2 * D].reshape(Bs, H, HEAD_DIM)
            v_s = qkv_s[:, 2 * D:].reshape(Bs, H, HEAD_DIM)
            att_s = _moba_decode(q_s, k_s, v_s, cache_k, cache_v, b, page_table, rel_bias)
            xp = _out_proj(att_p, w_out, xp)
            xs = _out_proj(att_s, w_out, xs)
            kp_rows.append(qkv_p[:, D:2 * D].reshape(Bp, Tp, H, HEAD_DIM))
            vp_rows.append(qkv_p[:, 2 * D:].reshape(Bp, Tp, H, HEAD_DIM))
            ks_rows.append(k_s.reshape(Bs, Ts, H, HEAD_DIM))
            vs_rows.append(v_s.reshape(Bs, Ts, H, HEAD_DIM))
        w_q = peer_w_q[layer].astype(BF16)
        u16 = peer_u[layer].astype(BF16)
        v16 = peer_v[layer].astype(BF16)
        xp = _peer(xp, norm_ffn[layer], w_q, peer_keys1[layer], peer_keys2[layer], u16, v16)
        xs = _peer(xs, norm_ffn[layer], w_q, peer_keys1[layer], peer_keys2[layer], u16, v16)
    return (xp.reshape(Bp, Tp, D), xs.reshape(Bs, Ts, D), jnp.stack(st_p), jnp.stack(st_s),
            jnp.stack(kp_rows), jnp.stack(vp_rows), jnp.stack(ks_rows), jnp.stack(vs_rows))
```

```python
import functools
import math

import numpy as np
import jax
import jax.numpy as jnp
from jax import lax
from jax.experimental import pallas as pl
from jax.experimental.pallas import tpu as pltpu

F32 = jnp.float32
BF16 = jnp.bfloat16
I32 = jnp.int32

D_MODEL = 1024
HEAD_DIM = 128
N_HEADS = D_MODEL // HEAD_DIM
PAGE_SIZE = 128
HGRN_CHUNK = 64
MOBA_BLOCK = 256
MOBA_TOPK = 3
REL_BUCKETS = 32
REL_MAX_DIST = 128
PEER_HEADS = 8
PEER_N_KEYS = 128
PEER_HALF = 128
PEER_TOPK = 16
PEER_PAIRS = PEER_HEADS * PEER_TOPK
RMS_EPS = 1e-6

LANES = 128
SUBLANES = 8
V7X_VMEM_BYTES = 64 * 1024 * 1024
NEG_BIG = -1e30
LOG2_E = 1.4426950408889634


def _vmem_limit(estimate_bytes):
    return int(min(max(2 * estimate_bytes, 16 * 1024 * 1024), V7X_VMEM_BYTES - 8 * 1024 * 1024))


def _params(sem, estimate_bytes):
    return pltpu.CompilerParams(dimension_semantics=sem, vmem_limit_bytes=_vmem_limit(estimate_bytes))


def _split2(x):
    hi = x.astype(BF16)
    lo = (x - hi.astype(F32)).astype(BF16)
    return hi, lo


def _split3(x):
    hi = x.astype(BF16)
    r = x - hi.astype(F32)
    mid = r.astype(BF16)
    lo = (r - mid.astype(F32)).astype(BF16)
    return hi, mid, lo


_NT = (((1,), (1,)), ((), ()))
_TN = (((0,), (0,)), ((), ()))


def _dot_nt(a, b):
    return lax.dot_general(a, b, _NT, preferred_element_type=F32)


def _dot_tn(a, b):
    return lax.dot_general(a, b, _TN, preferred_element_type=F32)


def _dot_nt_x3(a, b):
    ah, al = _split2(a)
    bh, bl = _split2(b)
    return _dot_nt(ah, bh) + _dot_nt(ah, bl) + _dot_nt(al, bh)


def _sigmoid(x):
    return 1.0 / (1.0 + jnp.exp(-x))


def _head_rms(y, gain):
    outs = []
    for h in range(y.shape[1] // HEAD_DIM):
        yh = y[:, h * HEAD_DIM:(h + 1) * HEAD_DIM]
        ms = jnp.mean(yh * yh, axis=-1, keepdims=True)
        outs.append(yh * lax.rsqrt(ms + RMS_EPS))
    return jnp.concatenate(outs, axis=-1) * gain


def _topk_rows(s, k, ids=None):
    if ids is None:
        ids = lax.broadcasted_iota(I32, s.shape, 0).astype(F32)
    vals, idxs = [], []
    cur = s
    for _ in range(k):
        m = jnp.max(cur, axis=0, keepdims=True)
        idx = jnp.min(jnp.where(cur == m, ids, jnp.inf), axis=0, keepdims=True)
        vals.append(m)
        idxs.append(idx)
        cur = jnp.where(ids == idx, -jnp.inf, cur)
    return jnp.concatenate(vals, axis=0), jnp.concatenate(idxs, axis=0)


def _norm_proj_kernel(x_ref, g_ref, w_ref, hg_ref, *rest, n_head_norm_tiles, emit_h, emit_bf16, emit_tail_t):
    rest = list(rest)
    wt_ref = rest.pop(0) if emit_tail_t else None
    y_ref = rest.pop(0)
    x = x_ref[...]
    ms = jnp.mean(x * x, axis=-1, keepdims=True)
    h = (x * lax.rsqrt(ms + RMS_EPS) * g_ref[...]).astype(BF16)
    y = jnp.dot(h, w_ref[...], preferred_element_type=F32)
    j = pl.program_id(1)
    if n_head_norm_tiles:
        y = jnp.where(j < n_head_norm_tiles, _head_rms(y, hg_ref[...]), y)
    y_ref[...] = y
    if emit_bf16:
        rest.pop(0)[...] = y.astype(BF16)
    if emit_h:
        rest.pop(0)[...] = h
    if emit_tail_t:
        yt_ref = rest.pop(0)

        @pl.when(j == pl.num_programs(1) - 1)
        def _():
            yt_ref[...] = _dot_nt(wt_ref[...], h).astype(BF16)


def _norm_proj(x, gain, w_bf16, *, head_gain=None, n_head_norm_tiles=0, emit_h=False, emit_bf16=False,
               emit_tail_t=False, tn=1024):
    T, D = x.shape
    N = w_bf16.shape[1]
    tm = min(T, 1024)
    assert T % tm == 0 and N % tn == 0
    n_tiles = N // tn
    if head_gain is None:
        head_gain = jnp.ones((N,), F32)
    operands = [x, gain.reshape(1, D), w_bf16, head_gain.reshape(1, N)]
    in_specs = [
        pl.BlockSpec((tm, D), lambda i, j: (i, 0)),
        pl.BlockSpec((1, D), lambda i, j: (0, 0)),
        pl.BlockSpec((D, tn), lambda i, j: (0, j)),
        pl.BlockSpec((1, tn), lambda i, j: (0, j)),
    ]
    out_shape = [jax.ShapeDtypeStruct((T, N), F32)]
    out_specs = [pl.BlockSpec((tm, tn), lambda i, j: (i, j))]
    if emit_bf16:
        out_shape.append(jax.ShapeDtypeStruct((T, N), BF16))
        out_specs.append(pl.BlockSpec((tm, tn), lambda i, j: (i, j)))
    if emit_h:
        out_shape.append(jax.ShapeDtypeStruct((T, D), BF16))
        out_specs.append(pl.BlockSpec((tm, D), lambda i, j: (i, 0)))
    if emit_tail_t:
        assert n_head_norm_tiles < n_tiles
        operands.append(w_bf16[:, N - tn:].T)
        in_specs.append(pl.BlockSpec((tn, D), lambda i, j: (0, 0)))
        out_shape.append(jax.ShapeDtypeStruct((tn, T), BF16))
        out_specs.append(pl.BlockSpec((tn, tm), lambda i, j: (0, i)))
    est = 2 * (tm * D * 4 + 2 * D * tn * 2 + tm * tn * 8 + tm * D * 2) + tm * tn * 8
    outs = pl.pallas_call(
        functools.partial(_norm_proj_kernel, n_head_norm_tiles=n_head_norm_tiles, emit_h=emit_h, emit_bf16=emit_bf16,
                          emit_tail_t=emit_tail_t),
        grid=(T // tm, n_tiles),
        in_specs=in_specs,
        out_specs=out_specs,
        out_shape=out_shape,
        compiler_params=_params(("parallel", "arbitrary"), est),
        name="norm_proj",
    )(*operands)
    return outs


def _out_proj_kernel(o_ref, gate_ref, gain_ref, w_ref, x_ref, y_ref, *, gated):
    o = o_ref[...]
    if gated:
        g = gate_ref[...]
        o = _head_rms(o, gain_ref[...]) * (g * _sigmoid(g))
    y_ref[...] = x_ref[...] + jnp.dot(o.astype(BF16), w_ref[...], preferred_element_type=F32)


def _out_proj(o, w_bf16, x, *, gate_src=None, gate_col=0, gain=None):
    T, D = o.shape
    tm = min(T, 512)
    assert T % tm == 0
    gated = gate_src is not None
    if not gated:
        gate_src, gain = o, jnp.ones((D,), F32)
    est = 2 * (4 * tm * D * 4 + D * D * 2)
    return pl.pallas_call(
        functools.partial(_out_proj_kernel, gated=gated),
        grid=(T // tm,),
        in_specs=[
            pl.BlockSpec((tm, D), lambda i: (i, 0)),
            pl.BlockSpec((tm, D), lambda i: (i, gate_col)),
            pl.BlockSpec((1, D), lambda i: (0, 0)),
            pl.BlockSpec((D, D), lambda i: (0, 0)),
            pl.BlockSpec((tm, D), lambda i: (i, 0)),
        ],
        out_specs=pl.BlockSpec((tm, D), lambda i: (i, 0)),
        out_shape=jax.ShapeDtypeStruct((T, D), F32),
        compiler_params=_params(("parallel",), est),
        name="out_proj",
    )(o, gate_src, gain.reshape(1, D), w_bf16, x)


HGRN_SAFE_DECAY = 80.0


def _hgrn_prompt_kernel(q_ref, fl_ref, v_ref, lb_ref, o_ref, s_out_ref, st_scr, b_scr, k_scr):
    c = pl.program_id(0)
    C = q_ref.shape[0]

    @pl.when(c == 0)
    def _():
        st_scr[...] = jnp.zeros_like(st_scr)

    lb = lb_ref[...]
    f = lb + (1.0 - lb) * _sigmoid(fl_ref[...])
    logf = jnp.log(f)
    q = q_ref[...]
    qq = q * _sigmoid(q)
    tri = (lax.broadcasted_iota(I32, (C, C), 0) >= lax.broadcasted_iota(I32, (C, C), 1)).astype(BF16)
    p0, p1, p2 = _split3(logf)
    b = (jnp.dot(tri, p0, preferred_element_type=F32) + jnp.dot(tri, p1, preferred_element_type=F32)
         + jnp.dot(tri, p2, preferred_element_type=F32))
    b_scr[...] = b
    k_scr[...] = 1.0 - f
    row = lax.broadcasted_iota(I32, (C, HEAD_DIM), 0)
    col = lax.broadcasted_iota(I32, (C, C), 1)
    causal = lax.broadcasted_iota(I32, (C, C), 0) >= col

    def head_step(h, hs, a_mat):
        bh = b_scr[:, hs]
        vh = v_ref[:, hs].astype(BF16)
        bl = bh[C - 1:C, :]
        qh = q_ref[:, hs]
        qh = qh * _sigmoid(qh)
        st = st_scr[h]
        inter = _dot_nt((qh * jnp.exp(bh)).astype(BF16), st.astype(BF16))
        intra = jnp.dot(a_mat.astype(BF16), vh, preferred_element_type=F32)
        o_ref[:, hs] = inter + intra
        k_dec = (k_scr[:, hs] * jnp.exp(bl - bh)).astype(BF16)
        st_scr[h] = st * jnp.exp(bl) + _dot_tn(vh, k_dec)

    safe = jnp.min(b[C - 1:C, :]) >= -HGRN_SAFE_DECAY

    @pl.when(safe)
    def _():
        for h in range(N_HEADS):
            hs = slice(h * HEAD_DIM, (h + 1) * HEAD_DIM)
            bh = b[:, hs]
            half = 0.5 * bh[C - 1:C, :]
            q_dec = (qq[:, hs] * jnp.exp(bh - half)).astype(BF16)
            k_inc = (k_scr[:, hs] * jnp.exp(half - bh)).astype(BF16)
            head_step(h, hs, jnp.where(causal, _dot_nt(q_dec, k_inc), 0.0))

    @pl.when(jnp.logical_not(safe))
    def _():
        for h in range(N_HEADS):
            hs = slice(h * HEAD_DIM, (h + 1) * HEAD_DIM)
            bh = b[:, hs]
            qh = qq[:, hs]

            def pair_cols(s8, a_mat):
                start = pl.multiple_of(s8 * SUBLANES, SUBLANES)
                b8 = b_scr[pl.ds(start, SUBLANES), hs]
                k8 = k_scr[pl.ds(start, SUBLANES), hs]
                for r in range(SUBLANES):
                    s = start + r
                    d = jnp.exp(jnp.where(row >= s, bh - b8[r:r + 1], -jnp.inf))
                    a_col = jnp.sum(qh * k8[r:r + 1] * d, axis=-1, keepdims=True)
                    a_mat = jnp.where(col == s, a_col, a_mat)
                return a_mat

            head_step(h, hs, lax.fori_loop(0, C // SUBLANES, pair_cols, jnp.zeros((C, C), F32)))

    @pl.when(c == pl.num_programs(0) - 1)
    def _():
        for h in range(N_HEADS):
            s_out_ref[h] = st_scr[h].T


def _hgrn_prompt(proj, lb):
    T = proj.shape[0]
    C = min(HGRN_CHUNK, T)
    assert T % C == 0 and C % SUBLANES == 0
    D = D_MODEL
    est = 2 * 4 * C * D * 4 + 3 * N_HEADS * HEAD_DIM * HEAD_DIM * 4 + 2 * C * D * 4
    return pl.pallas_call(
        _hgrn_prompt_kernel,
        grid=(T // C,),
        in_specs=[
            pl.BlockSpec((C, D), lambda c: (c, 0)),
            pl.BlockSpec((C, D), lambda c: (c, 1)),
            pl.BlockSpec((C, D), lambda c: (c, 2)),
            pl.BlockSpec((1, D), lambda c: (0, 0)),
        ],
        out_specs=[
            pl.BlockSpec((C, D), lambda c: (c, 0)),
            pl.BlockSpec((N_HEADS, HEAD_DIM, HEAD_DIM), lambda c: (0, 0, 0)),
        ],
        out_shape=[
            jax.ShapeDtypeStruct((T, D), F32),
            jax.ShapeDtypeStruct((N_HEADS, HEAD_DIM, HEAD_DIM), F32),
        ],
        scratch_shapes=[
            pltpu.VMEM((N_HEADS, HEAD_DIM, HEAD_DIM), F32),
            pltpu.VMEM((C, D), F32),
            pltpu.VMEM((C, D), F32),
        ],
        compiler_params=_params(("arbitrary",), est),
        name="hgrn_prompt",
    )(proj, proj, proj, lb.reshape(1, D))


def _hgrn_step_kernel(q_ref, fl_ref, lb_ref, v_ref, s_ref, o_ref, s_out_ref):
    lb = lb_ref[...]
    f = lb + (1.0 - lb) * _sigmoid(fl_ref[...])
    q = q_ref[...]
    qq = q * _sigmoid(q)
    s_new = s_ref[...] * f + (1.0 - f) * v_ref[...]
    s_out_ref[...] = s_new
    o_ref[...] = jnp.sum(s_new * qq, axis=1, keepdims=True)


def _hgrn_step(proj, lb, state):
    B = proj.shape[0]
    H, DK = N_HEADS, HEAD_DIM
    q = proj[:, :D_MODEL].reshape(B, H, DK, 1)
    fl = proj[:, D_MODEL:2 * D_MODEL].reshape(B, H, DK, 1)
    v = proj[:, 2 * D_MODEL:3 * D_MODEL].reshape(B, H, 1, DK)
    col_spec = pl.BlockSpec((None, H, DK, 1), lambda b: (b, 0, 0, 0))
    row_spec = pl.BlockSpec((None, H, 1, DK), lambda b: (b, 0, 0, 0))
    est = 2 * (3 * H * DK * LANES * 4 + 2 * H * DK * DK * 4)
    o, s_new = pl.pallas_call(
        _hgrn_step_kernel,
        grid=(B,),
        in_specs=[
            col_spec, col_spec,
            pl.BlockSpec((H, DK, 1), lambda b: (0, 0, 0)),
            row_spec,
            pl.BlockSpec((None, H, DK, DK), lambda b: (b, 0, 0, 0)),
        ],
        out_specs=[row_spec, pl.BlockSpec((None, H, DK, DK), lambda b: (b, 0, 0, 0))],
        out_shape=[jax.ShapeDtypeStruct((B, H, 1, DK), F32), jax.ShapeDtypeStruct(state.shape, F32)],
        compiler_params=_params(("parallel",), est),
        name="hgrn_step",
    )(q, fl, lb.reshape(H, DK, 1), v, state)
    return o.reshape(B, D_MODEL), s_new


def _rel_bucket_table():
    d = np.arange(REL_MAX_DIST + 1)
    max_exact = REL_BUCKETS // 2
    large = max_exact + (np.log(np.maximum(d, max_exact).astype(np.float32) / max_exact)
                         / math.log(REL_MAX_DIST / max_exact) * (REL_BUCKETS - max_exact)).astype(np.int32)
    large = np.minimum(large, REL_BUCKETS - 1)
    tab = np.where(d < max_exact, d, large).astype(np.int32)
    assert tab[REL_MAX_DIST] == REL_BUCKETS - 1
    return tab


def _block_mean_kernel(k_ref, o_ref):
    o_ref[...] = jnp.mean(k_ref[...], axis=0, keepdims=True)


def _block_mean(k_src, col_block, n_blocks):
    return pl.pallas_call(
        _block_mean_kernel,
        grid=(n_blocks,),
        in_specs=[pl.BlockSpec((MOBA_BLOCK, D_MODEL), lambda n: (n, col_block))],
        out_specs=pl.BlockSpec((None, 1, D_MODEL), lambda n: (n, 0, 0)),
        out_shape=jax.ShapeDtypeStruct((n_blocks, 1, D_MODEL), F32),
        compiler_params=_params(("parallel",), 4 * MOBA_BLOCK * D_MODEL * 4),
        name="moba_block_mean",
    )(k_src)


MOBA_FAR_GROUP = 4


def _moba_prompt_kernel(q_ref, kmean_ref, k_ref, vt_ref, bias_ref, far_ref, o_ref, sel_scr, m_scr, acc_scr,
                        s_scr, *, group):
    i = pl.program_id(1)
    tq = q_ref.shape[0]
    nb = kmean_ref.shape[0]
    q = q_ref[...]
    qs = (q * (HEAD_DIM ** -0.5 * LOG2_E)).astype(BF16)

    gate = _dot_nt_x3(kmean_ref[...], q)
    blk = lax.broadcasted_iota(I32, (nb, tq), 0)
    cur = jnp.where(blk < i, gate, -jnp.inf)
    sel = jnp.zeros((nb, tq), F32)
    for _ in range(MOBA_TOPK):
        m = jnp.max(cur, axis=0, keepdims=True)
        idx = jnp.min(jnp.where((cur == m) & (blk < i), blk, nb), axis=0, keepdims=True)
        hit = (blk == idx) & (cur > -jnp.inf)
        sel = jnp.where(hit, 1.0, sel)
        cur = jnp.where(blk == idx, -jnp.inf, cur)
    pad = sel_scr.shape[0] - nb
    sel_scr[...] = jnp.concatenate([sel, jnp.zeros((pad, tq), F32)], axis=0) if pad else sel
    sub = lax.broadcasted_iota(I32, (SUBLANES, tq), 0)

    def allowed(n):
        start = pl.multiple_of((n // SUBLANES) * SUBLANES, SUBLANES)
        rows = sel_scr[pl.ds(start, SUBLANES), :]
        return jnp.max(jnp.where(sub == n - start, rows, 0.0), axis=0, keepdims=True) > 0.0

    m_scr[...] = jnp.full_like(m_scr, NEG_BIG)
    acc_scr[...] = jnp.zeros_like(acc_scr)

    def attend(n, bias, allow):
        start = pl.multiple_of(n * MOBA_BLOCK, MOBA_BLOCK)
        s = _dot_nt(k_ref[pl.ds(start, MOBA_BLOCK), :], qs) + bias
        m_old = m_scr[...]
        m_new = jnp.maximum(m_old, jnp.max(jnp.where(allow, s, NEG_BIG), axis=0, keepdims=True))
        alpha = jnp.exp2(m_old - m_new)
        p = jnp.where(allow, jnp.exp2(s - m_new), 0.0)
        pv = jnp.dot(vt_ref[:, pl.ds(start, MOBA_BLOCK)], p.astype(BF16), preferred_element_type=F32)
        acc_scr[...] = alpha * acc_scr[...] + pv
        m_scr[...] = m_new

    far = far_ref[0:1, 0:1]

    n_groups_total = nb // group

    def group_start(ng):
        return pl.multiple_of(jnp.minimum(ng, n_groups_total - 1) * (group * MOBA_BLOCK), group * MOBA_BLOCK)

    def far_scores(ng, slot):
        s_scr[slot] = _dot_nt(k_ref[pl.ds(group_start(ng), group * MOBA_BLOCK), :], qs)

    def far_update(ng, slot):
        n0 = ng * group
        start = group_start(ng)
        parts = [s_scr[slot, g * MOBA_BLOCK:(g + 1) * MOBA_BLOCK, :] for g in range(group)]
        allow = [allowed(jnp.minimum(n0 + g, nb - 1)) & (n0 + g < i - 1) for g in range(group)]
        mx = jnp.where(allow[0], jnp.max(parts[0], axis=0, keepdims=True), NEG_BIG)
        for g in range(1, group):
            mx = jnp.maximum(mx, jnp.where(allow[g], jnp.max(parts[g], axis=0, keepdims=True), NEG_BIG))
        m_old = m_scr[...]
        m_new = jnp.maximum(m_old, mx + far)
        alpha = jnp.exp2(m_old - m_new)
        shift = m_new - far
        acc = alpha * acc_scr[...]
        for g in range(group):
            p = jnp.exp2(parts[g] - shift)
            start_g = pl.multiple_of(start + g * MOBA_BLOCK, MOBA_BLOCK)
            pv = jnp.dot(vt_ref[:, pl.ds(start_g, MOBA_BLOCK)], p.astype(BF16), preferred_element_type=F32)
            acc = acc + jnp.where(allow[g], pv, 0.0)
        acc_scr[...] = acc
        m_scr[...] = m_new

    n_groups = (i + group - 2) // group
    far_scores(0, 0)

    def far_pair(p, carry):
        far_scores(2 * p + 1, 1)
        far_update(2 * p, 0)
        far_scores(2 * p + 2, 0)
        far_update(2 * p + 1, 1)
        return carry

    lax.fori_loop(0, (n_groups + 1) // 2, far_pair, 0)

    causal = lax.broadcasted_iota(I32, (MOBA_BLOCK, tq), 0) <= lax.broadcasted_iota(I32, (MOBA_BLOCK, tq), 1)

    @pl.when(i == 0)
    def _():
        attend(0, bias_ref[0], causal)

    @pl.when(i >= 1)
    def _():
        start = pl.multiple_of((i - 1) * MOBA_BLOCK, MOBA_BLOCK)
        s = _dot_nt(k_ref[pl.ds(start, 2 * MOBA_BLOCK), :], qs)
        s_prev = s[:MOBA_BLOCK] + bias_ref[1]
        s_own = s[MOBA_BLOCK:] + bias_ref[0]
        al_prev = allowed(i - 1)
        mx = jnp.maximum(jnp.where(al_prev, jnp.max(s_prev, axis=0, keepdims=True), NEG_BIG),
                         jnp.max(jnp.where(causal, s_own, NEG_BIG), axis=0, keepdims=True))
        m_old = m_scr[...]
        m_new = jnp.maximum(m_old, mx)
        alpha = jnp.exp2(m_old - m_new)
        p_prev = jnp.exp2(s_prev - m_new).astype(BF16)
        p_own = jnp.where(causal, jnp.exp2(s_own - m_new), 0.0).astype(BF16)
        pv_prev = jnp.dot(vt_ref[:, pl.ds(start, MOBA_BLOCK)], p_prev, preferred_element_type=F32)
        start_own = pl.multiple_of(i * MOBA_BLOCK, MOBA_BLOCK)
        pv_own = jnp.dot(vt_ref[:, pl.ds(start_own, MOBA_BLOCK)], p_own, preferred_element_type=F32)
        acc_scr[...] = alpha * acc_scr[...] + jnp.where(al_prev, pv_prev, 0.0) + pv_own
        m_scr[...] = m_new
    o_ref[...] = (acc_scr[:HEAD_DIM, :] / acc_scr[HEAD_DIM:HEAD_DIM + 1, :]).T


def _moba_prompt(qkv, qkv_bf16, vt_bf16, rel_bias):
    T = qkv.shape[0]
    assert T % MOBA_BLOCK == 0
    nb = T // MOBA_BLOCK
    H = N_HEADS
    kmean = _block_mean(qkv, 1, nb).reshape(nb, D_MODEL)
    vt_ones = jnp.concatenate([vt_bf16.reshape(H, HEAD_DIM, T), jnp.ones((H, SUBLANES, T), BF16)], axis=1)
    tab = rel_bias[jnp.asarray(_rel_bucket_table())] * LOG2_E
    n = MOBA_BLOCK
    e = np.arange(2 * n)
    e_signed = np.where(e < n, e, e - 2 * n)
    w_own = tab[np.clip(e_signed, 0, REL_MAX_DIST)]
    w_prev = tab[np.clip(n + e_signed, 0, REL_MAX_DIST)]

    def toeplitz(w):
        return jnp.tile(w, n)[:n * (2 * n - 1)].reshape(n, 2 * n - 1)[:, :n]

    near = jnp.stack([jax.vmap(toeplitz, in_axes=1)(w_own), jax.vmap(toeplitz, in_axes=1)(w_prev)], axis=1)
    far = jnp.broadcast_to(tab[REL_MAX_DIST][:, None, None], (H, 8, LANES))
    group = math.gcd(nb, MOBA_FAR_GROUP)
    est = 2 * (2 * T * HEAD_DIM * 2 + 2 * MOBA_BLOCK * MOBA_BLOCK * 4) + (16 + 4 * group) * MOBA_BLOCK * MOBA_BLOCK * 4
    return pl.pallas_call(
        functools.partial(_moba_prompt_kernel, group=group),
        grid=(H, nb),
        in_specs=[
            pl.BlockSpec((MOBA_BLOCK, HEAD_DIM), lambda h, i: (i, h)),
            pl.BlockSpec((nb, HEAD_DIM), lambda h, i: (0, h)),
            pl.BlockSpec((T, HEAD_DIM), lambda h, i: (0, H + h)),
            pl.BlockSpec((None, HEAD_DIM + SUBLANES, T), lambda h, i: (h, 0, 0)),
            pl.BlockSpec((None, 2, MOBA_BLOCK, MOBA_BLOCK), lambda h, i: (h, 0, 0, 0)),
            pl.BlockSpec((None, 8, LANES), lambda h, i: (h, 0, 0)),
        ],
        out_specs=pl.BlockSpec((MOBA_BLOCK, HEAD_DIM), lambda h, i: (i, h)),
        out_shape=jax.ShapeDtypeStruct((T, D_MODEL), F32),
        scratch_shapes=[
            pltpu.VMEM((-(-nb // SUBLANES) * SUBLANES, MOBA_BLOCK), F32),
            pltpu.VMEM((1, MOBA_BLOCK), F32),
            pltpu.VMEM((HEAD_DIM + SUBLANES, MOBA_BLOCK), F32),
            pltpu.VMEM((2, group * MOBA_BLOCK, MOBA_BLOCK), F32),
        ],
        compiler_params=_params(("parallel", "arbitrary"), est),
        name="moba_prompt",
    )(qkv, kmean, qkv_bf16, vt_ones, near, far)


PAGES_PER_BLOCK = MOBA_BLOCK // PAGE_SIZE
MEAN_BLOCKS_PER_STEP = 4


def _paged_block_mean_kernel(pt_ref, *refs):
    page_refs, o_ref = refs[:-1], refs[-1]
    for n in range(len(page_refs) // PAGES_PER_BLOCK):
        tot = jnp.sum(page_refs[n * PAGES_PER_BLOCK][...], axis=0)
        for p in range(1, PAGES_PER_BLOCK):
            tot = tot + jnp.sum(page_refs[n * PAGES_PER_BLOCK + p][...], axis=0)
        o_ref[n] = tot * (1.0 / MOBA_BLOCK)


def _paged_block_mean(cache_k, layer, page_table):
    B, n_pages = page_table.shape
    assert n_pages % PAGES_PER_BLOCK == 0
    nb = n_pages // PAGES_PER_BLOCK
    bps = math.gcd(nb, MEAN_BLOCKS_PER_STEP)
    pps = bps * PAGES_PER_BLOCK
    page_shape = (None, None, PAGE_SIZE, N_HEADS, HEAD_DIM)

    def page_spec(p):
        return pl.BlockSpec(page_shape, lambda b, n, pt: (layer, pt[b * n_pages + n * pps + p], 0, 0, 0))

    grid_spec = pltpu.PrefetchScalarGridSpec(
        num_scalar_prefetch=1,
        grid=(B, nb // bps),
        in_specs=[page_spec(p) for p in range(pps)],
        out_specs=pl.BlockSpec((None, bps, N_HEADS, HEAD_DIM), lambda b, n, pt: (b, n, 0, 0)),
    )
    return pl.pallas_call(
        _paged_block_mean_kernel,
        grid_spec=grid_spec,
        out_shape=jax.ShapeDtypeStruct((B, nb, N_HEADS, HEAD_DIM), F32),
        compiler_params=_params(("parallel", "arbitrary"), 2 * pps * PAGE_SIZE * D_MODEL * 4),
        name="moba_paged_block_mean",
    )(page_table.reshape(-1), *([cache_k] * pps))


def _decode_gate_kernel(q_ref, kmean_ref, o_ref):
    nb = kmean_ref.shape[0]
    gate = jnp.sum(kmean_ref[...] * q_ref[...][None], axis=-1, keepdims=True)
    blk = lax.broadcasted_iota(I32, gate.shape, 0)
    cur = gate
    for r in range(MOBA_TOPK):
        m = jnp.max(cur, axis=0, keepdims=True)
        idx = jnp.min(jnp.where(cur == m, blk, nb), axis=0, keepdims=True)
        o_ref[r] = jnp.broadcast_to(idx[0], (N_HEADS, LANES))
        cur = jnp.where(blk == idx, -jnp.inf, cur)


def _decode_gate(q, kmean):
    B, nb = kmean.shape[:2]
    out = pl.pallas_call(
        _decode_gate_kernel,
        grid=(B,),
        in_specs=[
            pl.BlockSpec((None, N_HEADS, HEAD_DIM), lambda b: (b, 0, 0)),
            pl.BlockSpec((None, nb, N_HEADS, HEAD_DIM), lambda b: (b, 0, 0, 0)),
        ],
        out_specs=pl.BlockSpec((None, MOBA_TOPK, N_HEADS, LANES), lambda b: (b, 0, 0, 0)),
        out_shape=jax.ShapeDtypeStruct((B, MOBA_TOPK, N_HEADS, LANES), I32),
        compiler_params=_params(("parallel",), 4 * nb * N_HEADS * HEAD_DIM * 4),
        name="moba_decode_gate",
    )(q, kmean)
    return out[..., 0].transpose(0, 2, 1)


def _decode_attend_kernel(pages_ref, blocks_ref, q_ref, kn_ref, vn_ref, tab_ref, *refs, past_len):
    n_src = MOBA_TOPK * 2
    k_refs, v_refs, o_ref = refs[:n_src], refs[n_src:2 * n_src], refs[2 * n_src]
    b, h = pl.program_id(0), pl.program_id(1)
    head_rows = pl.ds(h, PAGE_SIZE, stride=N_HEADS)
    scale = HEAD_DIM ** -0.5
    q = q_ref[...]
    tab = tab_ref[...]
    row = lax.broadcasted_iota(I32, (PAGE_SIZE, 1), 0)
    lane = lax.broadcasted_iota(I32, (PAGE_SIZE, 2 * LANES), 1)
    logits = []
    for j in range(n_src):
        blk = blocks_ref[(b * N_HEADS + h) * MOBA_TOPK + j // 2]
        rel = past_len - (blk * MOBA_BLOCK + (j % 2) * PAGE_SIZE + row)
        bias = jnp.sum(jnp.where(lane == jnp.minimum(rel, REL_MAX_DIST), tab, 0.0), axis=-1, keepdims=True)
        logits.append(jnp.sum(k_refs[j][head_rows, :] * q, axis=-1, keepdims=True) * scale + bias)
    own = jnp.sum(kn_ref[...] * q, axis=-1, keepdims=True) * scale + tab[:, 0:1]
    m = own
    for s in logits:
        m = jnp.maximum(m, jnp.max(s, axis=0, keepdims=True))
    p_own = jnp.exp(own - m)
    l = p_own
    acc = p_own * vn_ref[...]
    for j in range(n_src):
        p = jnp.exp(logits[j] - m)
        l = l + jnp.sum(p, axis=0, keepdims=True)
        acc = acc + jnp.sum(p * v_refs[j][head_rows, :], axis=0, keepdims=True)
    o_ref[...] = acc / l


def _moba_decode(q, k_new, v_new, cache_k, cache_v, layer, page_table, rel_bias):
    B, n_pages = page_table.shape
    past_len = n_pages * PAGE_SIZE
    assert past_len % MOBA_BLOCK == 0 and past_len // MOBA_BLOCK >= MOBA_TOPK
    H = N_HEADS
    kmean = _paged_block_mean(cache_k, layer, page_table)
    blocks = _decode_gate(q, kmean)
    pages = jnp.take_along_axis(
        page_table[:, None, :], (blocks[..., None] * 2 + jnp.arange(2, dtype=I32)).reshape(B, H, -1), axis=2)
    n_src = MOBA_TOPK * 2
    tab = rel_bias[jnp.asarray(_rel_bucket_table())].T
    tab = jnp.pad(tab, ((0, 0), (0, 2 * LANES - tab.shape[1]))).reshape(H, 1, 2 * LANES)
    n_layers, n_pool = cache_k.shape[:2]
    ck = cache_k.reshape(n_layers, n_pool, PAGE_SIZE * H, HEAD_DIM)
    cv = cache_v.reshape(n_layers, n_pool, PAGE_SIZE * H, HEAD_DIM)

    def page_spec(j):
        return pl.BlockSpec((None, None, PAGE_SIZE * H, HEAD_DIM),
                            lambda b, h, pg, bk: (layer, pg[(b * H + h) * n_src + j], 0, 0))

    vec_spec = pl.BlockSpec((None, None, 1, HEAD_DIM), lambda b, h, pg, bk: (b, h, 0, 0))
    grid_spec = pltpu.PrefetchScalarGridSpec(
        num_scalar_prefetch=2,
        grid=(B, H),
        in_specs=[vec_spec, vec_spec, vec_spec,
                  pl.BlockSpec((None, 1, 2 * LANES), lambda b, h, pg, bk: (h, 0, 0))]
                 + [page_spec(j) for j in range(n_src)] * 2,
        out_specs=vec_spec,
    )
    r4 = lambda a: a.reshape(B, H, 1, HEAD_DIM)
    out = pl.pallas_call(
        functools.partial(_decode_attend_kernel, past_len=past_len),
        grid_spec=grid_spec,
        out_shape=jax.ShapeDtypeStruct((B, H, 1, HEAD_DIM), F32),
        compiler_params=_params(("parallel", "arbitrary"), 4 * n_src * PAGE_SIZE * HEAD_DIM * 4),
        name="moba_decode_attend",
    )(pages.reshape(-1), blocks.reshape(-1), r4(q), r4(k_new), r4(v_new), tab,
      *([ck] * n_src), *([cv] * n_src))
    return out.reshape(B, D_MODEL)


def _peer_select_kernel(qq_ref, k1_ref, k2_ref, a_ref, b_ref, g_ref):
    tq = qq_ref.shape[0]
    k1 = k1_ref[...]
    k2 = k2_ref[...]
    a_rows, b_rows, g_rows = [], [], []
    for h in range(PEER_HEADS):
        q1 = qq_ref[:, (2 * h) * PEER_HALF:(2 * h + 1) * PEER_HALF]
        q2 = qq_ref[:, (2 * h + 1) * PEER_HALF:(2 * h + 2) * PEER_HALF]
        v1, i1 = _topk_rows(_dot_nt_x3(k1, q1), PEER_TOPK)
        v2, i2 = _topk_rows(_dot_nt_x3(k2, q2), PEER_TOPK)
        cand, pos_ids = [], []
        for r in range(PEER_TOPK):
            cnt = PEER_TOPK // (r + 1)
            cand.append(v1[r:r + 1] + v2[:cnt])
            pos_ids.append((lax.broadcasted_iota(I32, (cnt, tq), 0) + r * PEER_TOPK).astype(F32))
        sc, pos = _topk_rows(jnp.concatenate(cand, axis=0), PEER_TOPK, jnp.concatenate(pos_ids, axis=0))
        pos = pos.astype(I32)
        r_k = pos >> 4
        c_k = pos & (PEER_TOPK - 1)
        a_k = jnp.zeros_like(sc)
        b_k = jnp.zeros_like(sc)
        for r in range(PEER_TOPK):
            a_k = jnp.where(r_k == r, i1[r:r + 1], a_k)
            b_k = jnp.where(c_k == r, i2[r:r + 1], b_k)
        e = jnp.exp(sc - sc[0:1])
        g_rows.append(e / jnp.sum(e, axis=0, keepdims=True))
        a_rows.append(a_k)
        b_rows.append(b_k)
    a_ref[...] = jnp.concatenate(a_rows, axis=0).T
    b_ref[...] = jnp.concatenate(b_rows, axis=0).T
    g_ref[...] = jnp.concatenate(g_rows, axis=0).T


def _peer_select(qq, keys1, keys2):
    T = qq.shape[0]
    tq = min(T, 256)
    assert T % tq == 0
    spec = pl.BlockSpec((tq, PEER_PAIRS), lambda i: (i, 0))
    key_spec = pl.BlockSpec((PEER_N_KEYS, PEER_HALF), lambda i: (0, 0))
    shape = jax.ShapeDtypeStruct((T, PEER_PAIRS), F32)
    return pl.pallas_call(
        _peer_select_kernel,
        grid=(T // tq,),
        in_specs=[pl.BlockSpec((tq, 2 * PEER_HEADS * PEER_HALF), lambda i: (i, 0)), key_spec, key_spec],
        out_specs=[spec, spec, spec],
        out_shape=[shape, shape, shape],
        compiler_params=_params(("parallel",), 4 * tq * 2048 * 4),
        name="peer_select",
    )(qq, keys1, keys2)


def _gelu_exact(x):
    return 0.5 * x * (1.0 + lax.erf(x * (2.0 ** -0.5)))


PEER_TOKEN_TILE = 512
PEER_BUILD_ROWS = 16
PEER_KEY1_PER_TILE = 8


def _peer_mix_kernel(a_ref, b_ref, g_ref, h_ref, ut_ref, v_ref, x_ref, y_ref, gmat_scr, acc_scr, stage_scr, *, groups):
    j = pl.program_id(1)
    tq = h_ref.shape[0]
    rows = PEER_BUILD_ROWS

    @pl.when(j == 0)
    def _():
        acc_scr[...] = jnp.zeros_like(acc_scr)
        key_b = lax.broadcasted_iota(I32, (PEER_N_KEYS, PEER_PAIRS), 0).astype(F32).astype(BF16)
        one_b = jnp.ones((PEER_N_KEYS, PEER_PAIRS), BF16)
        zero_b = jnp.zeros((PEER_N_KEYS, PEER_PAIRS), BF16)

        n_row_groups = tq // rows

        def token_mats(tg, slot):
            t0 = pl.multiple_of(jnp.minimum(tg, n_row_groups - 1) * rows, rows)
            a8 = a_ref[pl.ds(t0, rows), :]
            b8 = b_ref[pl.ds(t0, rows), :]
            g8 = g_ref[pl.ds(t0, rows), :].astype(BF16)
            for r in range(rows):
                a_hot = jnp.where(key_b == a8[r:r + 1].astype(BF16), one_b, zero_b)
                b_hot = jnp.where(key_b == b8[r:r + 1].astype(BF16), g8[r:r + 1], zero_b)
                stage_scr[slot, r] = _dot_nt(a_hot, b_hot).astype(BF16)

        def file_mats(tg, slot):
            t0 = pl.multiple_of(tg * rows, rows)
            gmat_scr[:, pl.ds(t0, rows), :] = jnp.swapaxes(stage_scr[slot], 0, 1)

        token_mats(0, 0)

        def build(p, carry):
            token_mats(2 * p + 1, 1)
            file_mats(2 * p, 0)
            token_mats(2 * p + 2, 0)
            file_mats(2 * p + 1, 1)
            return carry

        lax.fori_loop(0, n_row_groups // 2, build, 0)

    act = _gelu_exact(jnp.dot(h_ref[...], ut_ref[...], preferred_element_type=F32))
    w = jnp.concatenate([gmat_scr[j * groups + a] for a in range(groups)], axis=-1)
    z = (act * w.astype(F32)).astype(BF16)
    acc_scr[...] += jnp.dot(z, v_ref[...], preferred_element_type=F32)

    @pl.when(j == pl.num_programs(1) - 1)
    def _():
        y_ref[...] = x_ref[...] + acc_scr[...]


def _peer_mix(a_idx, b_idx, g, h_bf16, ut_bf16, v_bf16, x):
    T, D = x.shape
    E = v_bf16.shape[0]
    tq = min(T, PEER_TOKEN_TILE)
    groups = PEER_KEY1_PER_TILE
    te = groups * PEER_N_KEYS
    assert T % tq == 0 and tq % (2 * PEER_BUILD_ROWS) == 0 and E % te == 0 and E == PEER_N_KEYS * PEER_N_KEYS
    pair_spec = pl.BlockSpec((tq, PEER_PAIRS), lambda i, j: (i, 0))
    tok_spec = pl.BlockSpec((tq, D), lambda i, j: (i, 0))
    est = tq * PEER_N_KEYS * PEER_N_KEYS * 2 + 2 * (2 * te * D * 2 + 3 * tq * D * 4) + 4 * tq * te * 4
    return pl.pallas_call(
        functools.partial(_peer_mix_kernel, groups=groups),
        grid=(T // tq, E // te),
        in_specs=[
            pair_spec, pair_spec, pair_spec, tok_spec,
            pl.BlockSpec((D, te), lambda i, j: (0, j)),
            pl.BlockSpec((te, D), lambda i, j: (j, 0)),
            tok_spec,
        ],
        out_specs=tok_spec,
        out_shape=jax.ShapeDtypeStruct((T, D), F32),
        scratch_shapes=[
            pltpu.VMEM((PEER_N_KEYS, tq, PEER_N_KEYS), BF16),
            pltpu.VMEM((tq, D), F32),
            pltpu.VMEM((2, PEER_BUILD_ROWS, PEER_N_KEYS, PEER_N_KEYS), BF16),
        ],
        compiler_params=_params(("parallel", "arbitrary"), est),
        name="peer_mix",
    )(a_idx, b_idx, g, h_bf16, ut_bf16, v_bf16, x)


def _peer(x, norm_gain, w_q, keys1, keys2, ut_bf16, v_bf16):
    qq, h_bf16 = _norm_proj(x, norm_gain, w_q, emit_h=True)
    a_idx, b_idx, g = _peer_select(qq, keys1, keys2)
    return _peer_mix(a_idx, b_idx, g, h_bf16, ut_bf16, v_bf16, x)


def kernel(x_prompt, x_sample, state_hgrn, cache_k, cache_v, page_table, norm_mix, norm_ffn, hgrn_w_in, hgrn_lb,
           hgrn_o_norm, hgrn_w_out, moba_w_in, moba_q_norm, moba_k_norm, moba_w_out, rel_bias, peer_w_q,
           peer_keys1, peer_keys2, peer_u, peer_v):
    depth = norm_mix.shape[0]
    Bp, Tp, D = x_prompt.shape
    Bs, Ts, _ = x_sample.shape
    assert Bp == 1 and Ts == 1 and D == D_MODEL
    H = N_HEADS
    lbs = jnp.cumsum(jax.nn.softmax(hgrn_lb.astype(F32), axis=0), axis=0)
    xp = x_prompt.reshape(Tp, D)
    xs = x_sample.reshape(Bs, D)
    st_p, st_s, kp_rows, vp_rows, ks_rows, vs_rows = [], [], [], [], [], []
    for layer in range(depth):
        if layer % 2 == 0:
            a = layer // 2
            w_in = hgrn_w_in[a].astype(BF16)
            w_out = hgrn_w_out[a].astype(BF16)
            proj_p, = _norm_proj(xp, norm_mix[layer], w_in)
            proj_s, = _norm_proj(xs, norm_mix[layer], w_in)
            o_p, s_p = _hgrn_prompt(proj_p, lbs[a])
            o_s, s_s = _hgrn_step(proj_s, lbs[a], state_hgrn[a])
            xp = _out_proj(o_p, w_out, xp, gate_src=proj_p, gate_col=3, gain=hgrn_o_norm[a])
            xs = _out_proj(o_s, w_out, xs, gate_src=proj_s, gate_col=3, gain=hgrn_o_norm[a])
            st_p.append(s_p[None])
            st_s.append(s_s)
        else:
            b = layer // 2
            w_in = moba_w_in[b].astype(BF16)
            w_out = moba_w_out[b].astype(BF16)
            head_gain = jnp.concatenate([jnp.tile(moba_q_norm[b], H), jnp.tile(moba_k_norm[b], H), jnp.ones((D,), F32)])
            qkv_p, qkv_p16, vt_p16 = _norm_proj(xp, norm_mix[layer], w_in, head_gain=head_gain, n_head_norm_tiles=2,
                                                emit_bf16=True, emit_tail_t=True)
            qkv_s, = _norm_proj(xs, norm_mix[layer], w_in, head_gain=head_gain, n_head_norm_tiles=2)
            att_p = _moba_prompt(qkv_p, qkv_p16, vt_p16, rel_bias)
            q_s = qkv_s[:, :D].reshape(Bs, H, HEAD_DIM)
            k_s = qkv_s[:, D:2 * D].reshape(Bs, H, HEAD_DIM)
            v_s = qkv_s[:, 2 * D:].reshape(Bs, H, HEAD_DIM)
            att_s = _moba_decode(q_s, k_s, v_s, cache_k, cache_v, b, page_table, rel_bias)
            xp = _out_proj(att_p, w_out, xp)
            xs = _out_proj(att_s, w_out, xs)
            kp_rows.append(qkv_p[:, D:2 * D].reshape(Bp, Tp, H, HEAD_DIM))
            vp_rows.append(qkv_p[:, 2 * D:].reshape(Bp, Tp, H, HEAD_DIM))
            ks_rows.append(k_s.reshape(Bs, Ts, H, HEAD_DIM))
            vs_rows.append(v_s.reshape(Bs, Ts, H, HEAD_DIM))
        w_q = peer_w_q[layer].astype(BF16)
        u16 = peer_u[layer].astype(BF16).T
        v16 = peer_v[layer].astype(BF16)
        xp = _peer(xp, norm_ffn[layer], w_q, peer_keys1[layer], peer_keys2[layer], u16, v16)
        xs = _peer(xs, norm_ffn[layer], w_q, peer_keys1[layer], peer_keys2[layer], u16, v16)
    return (xp.reshape(Bp, Tp, D), xs.reshape(Bs, Ts, D), jnp.stack(st_p), jnp.stack(st_s),
            jnp.stack(kp_rows), jnp.stack(vp_rows), jnp.stack(ks_rows), jnp.stack(vs_rows))
```

```python
import functools
import math

import numpy as np
import jax
import jax.numpy as jnp
from jax import lax
from jax.experimental import pallas as pl
from jax.experimental.pallas import tpu as pltpu

F32 = jnp.float32
BF16 = jnp.bfloat16
I32 = jnp.int32

D_MODEL = 1024
HEAD_DIM = 128
N_HEADS = D_MODEL // HEAD_DIM
PAGE_SIZE = 128
HGRN_CHUNK = 64
MOBA_BLOCK = 256
MOBA_TOPK = 3
REL_BUCKETS = 32
REL_MAX_DIST = 128
PEER_HEADS = 8
PEER_N_KEYS = 128
PEER_HALF = 128
PEER_TOPK = 16
PEER_PAIRS = PEER_HEADS * PEER_TOPK
RMS_EPS = 1e-6

LANES = 128
SUBLANES = 8
V7X_VMEM_BYTES = 64 * 1024 * 1024
NEG_BIG = -1e30
LOG2_E = 1.4426950408889634


def _vmem_limit(estimate_bytes):
    return int(min(max(2 * estimate_bytes, 16 * 1024 * 1024), V7X_VMEM_BYTES - 8 * 1024 * 1024))


def _params(sem, estimate_bytes):
    return pltpu.CompilerParams(dimension_semantics=sem, vmem_limit_bytes=_vmem_limit(estimate_bytes))


def _split2(x):
    hi = x.astype(BF16)
    lo = (x - hi.astype(F32)).astype(BF16)
    return hi, lo


def _split3(x):
    hi = x.astype(BF16)
    r = x - hi.astype(F32)
    mid = r.astype(BF16)
    lo = (r - mid.astype(F32)).astype(BF16)
    return hi, mid, lo


_NT = (((1,), (1,)), ((), ()))
_TN = (((0,), (0,)), ((), ()))


def _dot_nt(a, b):
    return lax.dot_general(a, b, _NT, preferred_element_type=F32)


def _dot_tn(a, b):
    return lax.dot_general(a, b, _TN, preferred_element_type=F32)


def _dot_nt_x3(a, b):
    ah, al = _split2(a)
    bh, bl = _split2(b)
    return _dot_nt(ah, bh) + _dot_nt(ah, bl) + _dot_nt(al, bh)


def _sigmoid(x):
    return 1.0 / (1.0 + jnp.exp(-x))


def _head_rms(y, gain):
    outs = []
    for h in range(y.shape[1] // HEAD_DIM):
        yh = y[:, h * HEAD_DIM:(h + 1) * HEAD_DIM]
        ms = jnp.mean(yh * yh, axis=-1, keepdims=True)
        outs.append(yh * lax.rsqrt(ms + RMS_EPS))
    return jnp.concatenate(outs, axis=-1) * gain


def _topk_rows(s, k, ids=None):
    if ids is None:
        ids = lax.broadcasted_iota(I32, s.shape, 0).astype(F32)
    vals, idxs = [], []
    cur = s
    for _ in range(k):
        m = jnp.max(cur, axis=0, keepdims=True)
        idx = jnp.min(jnp.where(cur == m, ids, jnp.inf), axis=0, keepdims=True)
        vals.append(m)
        idxs.append(idx)
        cur = jnp.where(ids == idx, -jnp.inf, cur)
    return jnp.concatenate(vals, axis=0), jnp.concatenate(idxs, axis=0)


def _norm_cast(x_ref, g_ref):
    x = x_ref[...]
    ms = jnp.mean(x * x, axis=-1, keepdims=True)
    return (x * lax.rsqrt(ms + RMS_EPS) * g_ref[...]).astype(BF16)


def _norm_proj_kernel(x_ref, g_ref, w_ref, hg_ref, y_ref, *h_out, n_head_norm_tiles):
    h = _norm_cast(x_ref, g_ref)
    y = jnp.dot(h, w_ref[...], preferred_element_type=F32)
    if n_head_norm_tiles:
        y = jnp.where(pl.program_id(1) < n_head_norm_tiles, _head_rms(y, hg_ref[...]), y)
    y_ref[...] = y
    if h_out:
        h_out[0][...] = h


def _norm_proj(x, gain, w_bf16, *, head_gain=None, n_head_norm_tiles=0, emit_h=False, tn=1024):
    T, D = x.shape
    N = w_bf16.shape[1]
    tm = min(T, 1024)
    assert T % tm == 0 and N % tn == 0
    if head_gain is None:
        head_gain = jnp.ones((N,), F32)
    out_shape = [jax.ShapeDtypeStruct((T, N), F32)]
    out_specs = [pl.BlockSpec((tm, tn), lambda i, j: (i, j))]
    if emit_h:
        out_shape.append(jax.ShapeDtypeStruct((T, D), BF16))
        out_specs.append(pl.BlockSpec((tm, D), lambda i, j: (i, 0)))
    est = 2 * (tm * D * 4 + D * tn * 2 + tm * tn * 4 + tm * D * 2) + tm * tn * 8
    return pl.pallas_call(
        functools.partial(_norm_proj_kernel, n_head_norm_tiles=n_head_norm_tiles),
        grid=(T // tm, N // tn),
        in_specs=[
            pl.BlockSpec((tm, D), lambda i, j: (i, 0)),
            pl.BlockSpec((1, D), lambda i, j: (0, 0)),
            pl.BlockSpec((D, tn), lambda i, j: (0, j)),
            pl.BlockSpec((1, tn), lambda i, j: (0, j)),
        ],
        out_specs=out_specs,
        out_shape=out_shape,
        compiler_params=_params(("parallel", "arbitrary"), est),
        name="norm_proj",
    )(x, gain.reshape(1, D), w_bf16, head_gain.reshape(1, N))


VT_ONES_ROWS = 16


def _moba_qkv_kernel(x_ref, g_ref, w_ref, hg_ref, wvt_ref, q_ref, k_ref, v_ref, kb_ref, vt_ref, kmean_ref):
    j = pl.program_id(1)
    tm = x_ref.shape[0]
    h = _norm_cast(x_ref, g_ref)
    y = jnp.dot(h, w_ref[...], preferred_element_type=F32)

    @pl.when(j == 0)
    def _():
        q_ref[...] = _head_rms(y, hg_ref[...])

    @pl.when(j == 1)
    def _():
        k = _head_rms(y, hg_ref[...])
        k_ref[...] = k
        kb_ref[...] = k.astype(BF16)
        for b in range(tm // MOBA_BLOCK):
            kmean_ref[b] = jnp.mean(k[b * MOBA_BLOCK:(b + 1) * MOBA_BLOCK], axis=0, keepdims=True)

    @pl.when(j == 2)
    def _():
        v_ref[...] = y
        vt = _dot_nt(wvt_ref[...], h).astype(BF16)
        vt_ref[:, :HEAD_DIM, :] = vt.reshape(N_HEADS, HEAD_DIM, tm)
        vt_ref[:, HEAD_DIM:, :] = jnp.ones((N_HEADS, VT_ONES_ROWS, tm), BF16)


def _moba_qkv(x, gain, w_bf16, q_gain, k_gain):
    T, D = x.shape
    H = N_HEADS
    tm = min(T, 1024)
    assert T % tm == 0 and tm % MOBA_BLOCK == 0 and w_bf16.shape == (D, 3 * D)
    nb = T // MOBA_BLOCK
    head_gain = jnp.concatenate([jnp.tile(q_gain, H), jnp.tile(k_gain, H), jnp.ones((D,), F32)]).reshape(1, 3 * D)
    tok = pl.BlockSpec((tm, D), lambda i, j: (i, 0))
    rows = HEAD_DIM + VT_ONES_ROWS
    est = 2 * (tm * D * (4 + 3 * 4 + 2) + 2 * D * D * 2 + H * rows * tm * 2) + 2 * tm * D * 4
    q, k, v, kb, vt, kmean = pl.pallas_call(
        _moba_qkv_kernel,
        grid=(T // tm, 3),
        in_specs=[
            tok,
            pl.BlockSpec((1, D), lambda i, j: (0, 0)),
            pl.BlockSpec((D, D), lambda i, j: (0, j)),
            pl.BlockSpec((1, D), lambda i, j: (0, j)),
            pl.BlockSpec((D, D), lambda i, j: (0, 0)),
        ],
        out_specs=[tok, tok, tok, tok,
                   pl.BlockSpec((H, rows, tm), lambda i, j: (0, 0, i)),
                   pl.BlockSpec((tm // MOBA_BLOCK, 1, D), lambda i, j: (i, 0, 0))],
        out_shape=[jax.ShapeDtypeStruct((T, D), F32)] * 3 + [
            jax.ShapeDtypeStruct((T, D), BF16),
            jax.ShapeDtypeStruct((H, rows, T), BF16),
            jax.ShapeDtypeStruct((nb, 1, D), F32)],
        compiler_params=_params(("parallel", "arbitrary"), est),
        name="moba_qkv",
    )(x, gain.reshape(1, D), w_bf16, head_gain, w_bf16[:, 2 * D:].T)
    return q, k, v, kb, vt, kmean.reshape(nb, D)


def _out_proj_kernel(o_ref, gate_ref, gain_ref, w_ref, x_ref, y_ref, *, gated):
    o = o_ref[...]
    if gated:
        g = gate_ref[...]
        o = _head_rms(o, gain_ref[...]) * (g * _sigmoid(g))
    y_ref[...] = x_ref[...] + jnp.dot(o.astype(BF16), w_ref[...], preferred_element_type=F32)


def _out_proj(o, w_bf16, x, *, gate_src=None, gate_col=0, gain=None):
    T, D = o.shape
    tm = min(T, 512)
    assert T % tm == 0
    gated = gate_src is not None
    if not gated:
        gate_src, gain = o, jnp.ones((D,), F32)
    est = 2 * (4 * tm * D * 4 + D * D * 2)
    return pl.pallas_call(
        functools.partial(_out_proj_kernel, gated=gated),
        grid=(T // tm,),
        in_specs=[
            pl.BlockSpec((tm, D), lambda i: (i, 0)),
            pl.BlockSpec((tm, D), lambda i: (i, gate_col)),
            pl.BlockSpec((1, D), lambda i: (0, 0)),
            pl.BlockSpec((D, D), lambda i: (0, 0)),
            pl.BlockSpec((tm, D), lambda i: (i, 0)),
        ],
        out_specs=pl.BlockSpec((tm, D), lambda i: (i, 0)),
        out_shape=jax.ShapeDtypeStruct((T, D), F32),
        compiler_params=_params(("parallel",), est),
        name="out_proj",
    )(o, gate_src, gain.reshape(1, D), w_bf16, x)


HGRN_SAFE_DECAY = 80.0


def _hgrn_prompt_kernel(q_ref, fl_ref, v_ref, lb_ref, o_ref, s_out_ref, st_scr, b_scr, k_scr):
    c = pl.program_id(0)
    C = q_ref.shape[0]

    @pl.when(c == 0)
    def _():
        st_scr[...] = jnp.zeros_like(st_scr)

    lb = lb_ref[...]
    f = lb + (1.0 - lb) * _sigmoid(fl_ref[...])
    logf = jnp.log(f)
    q = q_ref[...]
    qq = q * _sigmoid(q)
    tri = (lax.broadcasted_iota(I32, (C, C), 0) >= lax.broadcasted_iota(I32, (C, C), 1)).astype(BF16)
    p0, p1, p2 = _split3(logf)
    b = (jnp.dot(tri, p0, preferred_element_type=F32) + jnp.dot(tri, p1, preferred_element_type=F32)
         + jnp.dot(tri, p2, preferred_element_type=F32))
    b_scr[...] = b
    k_scr[...] = 1.0 - f
    row = lax.broadcasted_iota(I32, (C, HEAD_DIM), 0)
    col = lax.broadcasted_iota(I32, (C, C), 1)
    causal = lax.broadcasted_iota(I32, (C, C), 0) >= col

    def head_step(h, hs, a_mat):
        bh = b_scr[:, hs]
        vh = v_ref[:, hs].astype(BF16)
        bl = bh[C - 1:C, :]
        qh = q_ref[:, hs]
        qh = qh * _sigmoid(qh)
        st = st_scr[h]
        inter = _dot_nt((qh * jnp.exp(bh)).astype(BF16), st.astype(BF16))
        intra = jnp.dot(a_mat.astype(BF16), vh, preferred_element_type=F32)
        o_ref[:, hs] = inter + intra
        k_dec = (k_scr[:, hs] * jnp.exp(bl - bh)).astype(BF16)
        st_scr[h] = st * jnp.exp(bl) + _dot_tn(vh, k_dec)

    safe = jnp.min(b[C - 1:C, :]) >= -HGRN_SAFE_DECAY

    @pl.when(safe)
    def _():
        for h in range(N_HEADS):
            hs = slice(h * HEAD_DIM, (h + 1) * HEAD_DIM)
            bh = b[:, hs]
            half = 0.5 * bh[C - 1:C, :]
            q_dec = (qq[:, hs] * jnp.exp(bh - half)).astype(BF16)
            k_inc = (k_scr[:, hs] * jnp.exp(half - bh)).astype(BF16)
            head_step(h, hs, jnp.where(causal, _dot_nt(q_dec, k_inc), 0.0))

    @pl.when(jnp.logical_not(safe))
    def _():
        for h in range(N_HEADS):
            hs = slice(h * HEAD_DIM, (h + 1) * HEAD_DIM)
            bh = b[:, hs]
            qh = qq[:, hs]

            def pair_cols(s8, a_mat):
                start = pl.multiple_of(s8 * SUBLANES, SUBLANES)
                b8 = b_scr[pl.ds(start, SUBLANES), hs]
                k8 = k_scr[pl.ds(start, SUBLANES), hs]
                for r in range(SUBLANES):
                    s = start + r
                    d = jnp.exp(jnp.where(row >= s, bh - b8[r:r + 1], -jnp.inf))
                    a_col = jnp.sum(qh * k8[r:r + 1] * d, axis=-1, keepdims=True)
                    a_mat = jnp.where(col == s, a_col, a_mat)
                return a_mat

            head_step(h, hs, lax.fori_loop(0, C // SUBLANES, pair_cols, jnp.zeros((C, C), F32)))

    @pl.when(c == pl.num_programs(0) - 1)
    def _():
        for h in range(N_HEADS):
            s_out_ref[h] = st_scr[h].T


def _hgrn_prompt(proj, lb):
    T = proj.shape[0]
    C = min(HGRN_CHUNK, T)
    assert T % C == 0 and C % SUBLANES == 0
    D = D_MODEL
    est = 2 * 4 * C * D * 4 + 3 * N_HEADS * HEAD_DIM * HEAD_DIM * 4 + 2 * C * D * 4
    return pl.pallas_call(
        _hgrn_prompt_kernel,
        grid=(T // C,),
        in_specs=[
            pl.BlockSpec((C, D), lambda c: (c, 0)),
            pl.BlockSpec((C, D), lambda c: (c, 1)),
            pl.BlockSpec((C, D), lambda c: (c, 2)),
            pl.BlockSpec((1, D), lambda c: (0, 0)),
        ],
        out_specs=[
            pl.BlockSpec((C, D), lambda c: (c, 0)),
            pl.BlockSpec((N_HEADS, HEAD_DIM, HEAD_DIM), lambda c: (0, 0, 0)),
        ],
        out_shape=[
            jax.ShapeDtypeStruct((T, D), F32),
            jax.ShapeDtypeStruct((N_HEADS, HEAD_DIM, HEAD_DIM), F32),
        ],
        scratch_shapes=[
            pltpu.VMEM((N_HEADS, HEAD_DIM, HEAD_DIM), F32),
            pltpu.VMEM((C, D), F32),
            pltpu.VMEM((C, D), F32),
        ],
        compiler_params=_params(("arbitrary",), est),
        name="hgrn_prompt",
    )(proj, proj, proj, lb.reshape(1, D))


def _hgrn_step_kernel(q_ref, fl_ref, lb_ref, v_ref, s_ref, o_ref, s_out_ref):
    lb = lb_ref[...]
    f = lb + (1.0 - lb) * _sigmoid(fl_ref[...])
    q = q_ref[...]
    qq = q * _sigmoid(q)
    s_new = s_ref[...] * f + (1.0 - f) * v_ref[...]
    s_out_ref[...] = s_new
    o_ref[...] = jnp.sum(s_new * qq, axis=1, keepdims=True)


def _hgrn_step(proj, lb, state):
    B = proj.shape[0]
    H, DK = N_HEADS, HEAD_DIM
    q = proj[:, :D_MODEL].reshape(B, H, DK, 1)
    fl = proj[:, D_MODEL:2 * D_MODEL].reshape(B, H, DK, 1)
    v = proj[:, 2 * D_MODEL:3 * D_MODEL].reshape(B, H, 1, DK)
    col_spec = pl.BlockSpec((None, H, DK, 1), lambda b: (b, 0, 0, 0))
    row_spec = pl.BlockSpec((None, H, 1, DK), lambda b: (b, 0, 0, 0))
    est = 2 * (3 * H * DK * LANES * 4 + 2 * H * DK * DK * 4)
    o, s_new = pl.pallas_call(
        _hgrn_step_kernel,
        grid=(B,),
        in_specs=[
            col_spec, col_spec,
            pl.BlockSpec((H, DK, 1), lambda b: (0, 0, 0)),
            row_spec,
            pl.BlockSpec((None, H, DK, DK), lambda b: (b, 0, 0, 0)),
        ],
        out_specs=[row_spec, pl.BlockSpec((None, H, DK, DK), lambda b: (b, 0, 0, 0))],
        out_shape=[jax.ShapeDtypeStruct((B, H, 1, DK), F32), jax.ShapeDtypeStruct(state.shape, F32)],
        compiler_params=_params(("parallel",), est),
        name="hgrn_step",
    )(q, fl, lb.reshape(H, DK, 1), v, state)
    return o.reshape(B, D_MODEL), s_new


def _rel_bucket_table():
    d = np.arange(REL_MAX_DIST + 1)
    max_exact = REL_BUCKETS // 2
    large = max_exact + (np.log(np.maximum(d, max_exact).astype(np.float32) / max_exact)
                         / math.log(REL_MAX_DIST / max_exact) * (REL_BUCKETS - max_exact)).astype(np.int32)
    large = np.minimum(large, REL_BUCKETS - 1)
    tab = np.where(d < max_exact, d, large).astype(np.int32)
    assert tab[REL_MAX_DIST] == REL_BUCKETS - 1
    return tab


MOBA_FAR_GROUP = 4


def _moba_prompt_kernel(q_ref, kmean_ref, k_ref, vt_ref, bias_ref, far_ref, o_ref, sel_scr, m_scr, acc_scr,
                        s_scr, *, group):
    i = pl.program_id(1)
    tq = q_ref.shape[0]
    nb = kmean_ref.shape[0]
    q = q_ref[...]
    qs = (q * (HEAD_DIM ** -0.5 * LOG2_E)).astype(BF16)

    gate = _dot_nt_x3(kmean_ref[...], q)
    blk = lax.broadcasted_iota(I32, (nb, tq), 0)
    cur = jnp.where(blk < i, gate, -jnp.inf)
    sel = jnp.zeros((nb, tq), F32)
    for _ in range(MOBA_TOPK):
        m = jnp.max(cur, axis=0, keepdims=True)
        idx = jnp.min(jnp.where((cur == m) & (blk < i), blk, nb), axis=0, keepdims=True)
        hit = (blk == idx) & (cur > -jnp.inf)
        sel = jnp.where(hit, 1.0, sel)
        cur = jnp.where(blk == idx, -jnp.inf, cur)
    pad = sel_scr.shape[0] - nb
    sel_scr[...] = jnp.concatenate([sel, jnp.zeros((pad, tq), F32)], axis=0) if pad else sel
    sub = lax.broadcasted_iota(I32, (SUBLANES, tq), 0)

    def allowed(n):
        start = pl.multiple_of((n // SUBLANES) * SUBLANES, SUBLANES)
        rows = sel_scr[pl.ds(start, SUBLANES), :]
        return jnp.max(jnp.where(sub == n - start, rows, 0.0), axis=0, keepdims=True) > 0.0

    m_scr[...] = jnp.full_like(m_scr, NEG_BIG)
    acc_scr[...] = jnp.zeros_like(acc_scr)

    def attend(n, bias, allow):
        start = pl.multiple_of(n * MOBA_BLOCK, MOBA_BLOCK)
        s = _dot_nt(k_ref[pl.ds(start, MOBA_BLOCK), :], qs) + bias
        m_old = m_scr[...]
        m_new = jnp.maximum(m_old, jnp.max(jnp.where(allow, s, NEG_BIG), axis=0, keepdims=True))
        alpha = jnp.exp2(m_old - m_new)
        p = jnp.where(allow, jnp.exp2(s - m_new), 0.0)
        pv = jnp.dot(vt_ref[:, pl.ds(start, MOBA_BLOCK)], p.astype(BF16), preferred_element_type=F32)
        acc_scr[...] = alpha * acc_scr[...] + pv
        m_scr[...] = m_new

    far = far_ref[0:1, 0:1]

    n_groups_total = nb // group

    def group_start(ng):
        return pl.multiple_of(jnp.minimum(ng, n_groups_total - 1) * (group * MOBA_BLOCK), group * MOBA_BLOCK)

    def far_scores(ng, slot):
        s_scr[slot] = _dot_nt(k_ref[pl.ds(group_start(ng), group * MOBA_BLOCK), :], qs)

    def far_update(ng, slot):
        n0 = ng * group
        start = group_start(ng)
        parts = [s_scr[slot, g * MOBA_BLOCK:(g + 1) * MOBA_BLOCK, :] for g in range(group)]
        allow = [allowed(jnp.minimum(n0 + g, nb - 1)) & (n0 + g < i - 1) for g in range(group)]
        mx = jnp.where(allow[0], jnp.max(parts[0], axis=0, keepdims=True), NEG_BIG)
        for g in range(1, group):
            mx = jnp.maximum(mx, jnp.where(allow[g], jnp.max(parts[g], axis=0, keepdims=True), NEG_BIG))
        m_old = m_scr[...]
        m_new = jnp.maximum(m_old, mx + far)
        alpha = jnp.exp2(m_old - m_new)
        shift = m_new - far
        acc = alpha * acc_scr[...]
        for g in range(group):
            p = jnp.exp2(parts[g] - shift)
            start_g = pl.multiple_of(start + g * MOBA_BLOCK, MOBA_BLOCK)
            pv = jnp.dot(vt_ref[:, pl.ds(start_g, MOBA_BLOCK)], p.astype(BF16), preferred_element_type=F32)
            acc = acc + jnp.where(allow[g], pv, 0.0)
        acc_scr[...] = acc
        m_scr[...] = m_new

    n_groups = (i + group - 2) // group
    far_scores(0, 0)

    def far_pair(p, carry):
        far_scores(2 * p + 1, 1)
        far_update(2 * p, 0)
        far_scores(2 * p + 2, 0)
        far_update(2 * p + 1, 1)
        return carry

    lax.fori_loop(0, (n_groups + 1) // 2, far_pair, 0)

    causal = lax.broadcasted_iota(I32, (MOBA_BLOCK, tq), 0) <= lax.broadcasted_iota(I32, (MOBA_BLOCK, tq), 1)

    @pl.when(i == 0)
    def _():
        attend(0, bias_ref[0], causal)

    @pl.when(i >= 1)
    def _():
        start = pl.multiple_of((i - 1) * MOBA_BLOCK, MOBA_BLOCK)
        s = _dot_nt(k_ref[pl.ds(start, 2 * MOBA_BLOCK), :], qs)
        s_prev = s[:MOBA_BLOCK] + bias_ref[1]
        s_own = s[MOBA_BLOCK:] + bias_ref[0]
        al_prev = allowed(i - 1)
        mx = jnp.maximum(jnp.where(al_prev, jnp.max(s_prev, axis=0, keepdims=True), NEG_BIG),
                         jnp.max(jnp.where(causal, s_own, NEG_BIG), axis=0, keepdims=True))
        m_old = m_scr[...]
        m_new = jnp.maximum(m_old, mx)
        alpha = jnp.exp2(m_old - m_new)
        p_prev = jnp.exp2(s_prev - m_new).astype(BF16)
        p_own = jnp.where(causal, jnp.exp2(s_own - m_new), 0.0).astype(BF16)
        pv_prev = jnp.dot(vt_ref[:, pl.ds(start, MOBA_BLOCK)], p_prev, preferred_element_type=F32)
        start_own = pl.multiple_of(i * MOBA_BLOCK, MOBA_BLOCK)
        pv_own = jnp.dot(vt_ref[:, pl.ds(start_own, MOBA_BLOCK)], p_own, preferred_element_type=F32)
        acc_scr[...] = alpha * acc_scr[...] + jnp.where(al_prev, pv_prev, 0.0) + pv_own
        m_scr[...] = m_new
    o_ref[...] = (acc_scr[:HEAD_DIM, :] / acc_scr[HEAD_DIM:HEAD_DIM + 1, :]).T


def _moba_prompt(q, k_bf16, vt_ones, kmean, rel_bias):
    T = q.shape[0]
    assert T % MOBA_BLOCK == 0
    nb = T // MOBA_BLOCK
    H = N_HEADS
    tab =rel_bias[jnp.asarray(_rel_bucket_table())] * LOG2_E
    n = MOBA_BLOCK
    e = np.arange(2 * n)
    e_signed = np.where(e < n, e, e - 2 * n)
    w_own = tab[np.clip(e_signed, 0, REL_MAX_DIST)]
    w_prev = tab[np.clip(n + e_signed, 0, REL_MAX_DIST)]

    def toeplitz(w):
        return jnp.tile(w, n)[:n * (2 * n - 1)].reshape(n, 2 * n - 1)[:, :n]

    near = jnp.stack([jax.vmap(toeplitz, in_axes=1)(w_own), jax.vmap(toeplitz, in_axes=1)(w_prev)], axis=1)
    far = jnp.broadcast_to(tab[REL_MAX_DIST][:, None, None], (H, 8, LANES))
    group = math.gcd(nb, MOBA_FAR_GROUP)
    est = 2 * (2 * T * HEAD_DIM * 2 + 2 * MOBA_BLOCK * MOBA_BLOCK * 4) + (16 + 4 * group) * MOBA_BLOCK * MOBA_BLOCK * 4
    return pl.pallas_call(
        functools.partial(_moba_prompt_kernel, group=group),
        grid=(H, nb),
        in_specs=[
            pl.BlockSpec((MOBA_BLOCK, HEAD_DIM), lambda h, i: (i, h)),
            pl.BlockSpec((nb, HEAD_DIM), lambda h, i: (0, h)),
            pl.BlockSpec((T, HEAD_DIM), lambda h, i: (0, h)),
            pl.BlockSpec((None, HEAD_DIM + VT_ONES_ROWS, T), lambda h, i: (h, 0, 0)),
            pl.BlockSpec((None, 2, MOBA_BLOCK, MOBA_BLOCK), lambda h, i: (h, 0, 0, 0)),
            pl.BlockSpec((None, 8, LANES), lambda h, i: (h, 0, 0)),
        ],
        out_specs=pl.BlockSpec((MOBA_BLOCK, HEAD_DIM), lambda h, i: (i, h)),
        out_shape=jax.ShapeDtypeStruct((T, D_MODEL), F32),
        scratch_shapes=[
            pltpu.VMEM((-(-nb // SUBLANES) * SUBLANES, MOBA_BLOCK), F32),
            pltpu.VMEM((1, MOBA_BLOCK), F32),
            pltpu.VMEM((HEAD_DIM + VT_ONES_ROWS, MOBA_BLOCK), F32),
            pltpu.VMEM((2, group * MOBA_BLOCK, MOBA_BLOCK), F32),
        ],
        compiler_params=_params(("parallel", "arbitrary"), est),
        name="moba_prompt",
    )(q, kmean, k_bf16, vt_ones, near, far)


PAGES_PER_BLOCK = MOBA_BLOCK // PAGE_SIZE
MEAN_BLOCKS_PER_STEP = 4


def _paged_block_mean_kernel(pt_ref, *refs):
    page_refs, o_ref = refs[:-1], refs[-1]
    for n in range(len(page_refs) // PAGES_PER_BLOCK):
        tot = jnp.sum(page_refs[n * PAGES_PER_BLOCK][...], axis=0)
        for p in range(1, PAGES_PER_BLOCK):
            tot = tot + jnp.sum(page_refs[n * PAGES_PER_BLOCK + p][...], axis=0)
        o_ref[n] = tot * (1.0 / MOBA_BLOCK)


def _paged_block_mean(cache_k, layer, page_table):
    B, n_pages = page_table.shape
    assert n_pages % PAGES_PER_BLOCK == 0
    nb = n_pages // PAGES_PER_BLOCK
    bps = math.gcd(nb, MEAN_BLOCKS_PER_STEP)
    pps = bps * PAGES_PER_BLOCK
    page_shape = (None, None, PAGE_SIZE, N_HEADS, HEAD_DIM)

    def page_spec(p):
        return pl.BlockSpec(page_shape, lambda b, n, pt: (layer, pt[b * n_pages + n * pps + p], 0, 0, 0))

    grid_spec = pltpu.PrefetchScalarGridSpec(
        num_scalar_prefetch=1,
        grid=(B, nb // bps),
        in_specs=[page_spec(p) for p in range(pps)],
        out_specs=pl.BlockSpec((None, bps, N_HEADS, HEAD_DIM), lambda b, n, pt: (b, n, 0, 0)),
    )
    return pl.pallas_call(
        _paged_block_mean_kernel,
        grid_spec=grid_spec,
        out_shape=jax.ShapeDtypeStruct((B, nb, N_HEADS, HEAD_DIM), F32),
        compiler_params=_params(("parallel", "arbitrary"), 2 * pps * PAGE_SIZE * D_MODEL * 4),
        name="moba_paged_block_mean",
    )(page_table.reshape(-1), *([cache_k] * pps))


def _decode_gate_kernel(q_ref, kmean_ref, o_ref):
    nb = kmean_ref.shape[0]
    gate = jnp.sum(kmean_ref[...] * q_ref[...][None], axis=-1, keepdims=True)
    blk = lax.broadcasted_iota(I32, gate.shape, 0)
    cur = gate
    for r in range(MOBA_TOPK):
        m = jnp.max(cur, axis=0, keepdims=True)
        idx = jnp.min(jnp.where(cur == m, blk, nb), axis=0, keepdims=True)
        o_ref[r] = jnp.broadcast_to(idx[0], (N_HEADS, LANES))
        cur = jnp.where(blk == idx, -jnp.inf, cur)


def _decode_gate(q, kmean):
    B, nb = kmean.shape[:2]
    out = pl.pallas_call(
        _decode_gate_kernel,
        grid=(B,),
        in_specs=[
            pl.BlockSpec((None, N_HEADS, HEAD_DIM), lambda b: (b, 0, 0)),
            pl.BlockSpec((None, nb, N_HEADS, HEAD_DIM), lambda b: (b, 0, 0, 0)),
        ],
        out_specs=pl.BlockSpec((None, MOBA_TOPK, N_HEADS, LANES), lambda b: (b, 0, 0, 0)),
        out_shape=jax.ShapeDtypeStruct((B, MOBA_TOPK, N_HEADS, LANES), I32),
        compiler_params=_params(("parallel",), 4 * nb * N_HEADS * HEAD_DIM * 4),
        name="moba_decode_gate",
    )(q, kmean)
    return out[..., 0].transpose(0, 2, 1)


def _decode_attend_kernel(pages_ref, blocks_ref, q_ref, kn_ref, vn_ref, tab_ref, *refs, past_len):
    n_src = MOBA_TOPK * 2
    k_refs, v_refs, o_ref = refs[:n_src], refs[n_src:2 * n_src], refs[2 * n_src]
    b, h = pl.program_id(0), pl.program_id(1)
    head_rows = pl.ds(h, PAGE_SIZE, stride=N_HEADS)
    scale = HEAD_DIM ** -0.5
    q = q_ref[...]
    tab = tab_ref[...]
    row = lax.broadcasted_iota(I32, (PAGE_SIZE, 1), 0)
    lane = lax.broadcasted_iota(I32, (PAGE_SIZE, 2 * LANES), 1)
    logits = []
    for j in range(n_src):
        blk = blocks_ref[(b * N_HEADS + h) * MOBA_TOPK + j // 2]
        rel = past_len - (blk * MOBA_BLOCK + (j % 2) * PAGE_SIZE + row)
        bias = jnp.sum(jnp.where(lane == jnp.minimum(rel, REL_MAX_DIST), tab, 0.0), axis=-1, keepdims=True)
        logits.append(jnp.sum(k_refs[j][head_rows, :] * q, axis=-1, keepdims=True) * scale + bias)
    own = jnp.sum(kn_ref[...] * q, axis=-1, keepdims=True) * scale + tab[:, 0:1]
    m = own
    for s in logits:
        m = jnp.maximum(m, jnp.max(s, axis=0, keepdims=True))
    p_own = jnp.exp(own - m)
    l = p_own
    acc = p_own * vn_ref[...]
    for j in range(n_src):
        p = jnp.exp(logits[j] - m)
        l = l + jnp.sum(p, axis=0, keepdims=True)
        acc = acc + jnp.sum(p * v_refs[j][head_rows, :], axis=0, keepdims=True)
    o_ref[...] = acc / l


def _moba_decode(q, k_new, v_new, cache_k, cache_v, layer, page_table, rel_bias):
    B, n_pages = page_table.shape
    past_len = n_pages * PAGE_SIZE
    assert past_len % MOBA_BLOCK == 0 and past_len // MOBA_BLOCK >= MOBA_TOPK
    H = N_HEADS
    kmean = _paged_block_mean(cache_k, layer, page_table)
    blocks = _decode_gate(q, kmean)
    pages = jnp.take_along_axis(
        page_table[:, None, :], (blocks[..., None] * 2 + jnp.arange(2, dtype=I32)).reshape(B, H, -1), axis=2)
    n_src = MOBA_TOPK * 2
    tab = rel_bias[jnp.asarray(_rel_bucket_table())].T
    tab = jnp.pad(tab, ((0, 0), (0, 2 * LANES - tab.shape[1]))).reshape(H, 1, 2 * LANES)
    n_layers, n_pool = cache_k.shape[:2]
    ck = cache_k.reshape(n_layers, n_pool, PAGE_SIZE * H, HEAD_DIM)
    cv = cache_v.reshape(n_layers, n_pool, PAGE_SIZE * H, HEAD_DIM)

    def page_spec(j):
        return pl.BlockSpec((None, None, PAGE_SIZE * H, HEAD_DIM),
                            lambda b, h, pg, bk: (layer, pg[(b * H + h) * n_src + j], 0, 0))

    vec_spec = pl.BlockSpec((None, None, 1, HEAD_DIM), lambda b, h, pg, bk: (b, h, 0, 0))
    grid_spec = pltpu.PrefetchScalarGridSpec(
        num_scalar_prefetch=2,
        grid=(B, H),
        in_specs=[vec_spec, vec_spec, vec_spec,
                  pl.BlockSpec((None, 1, 2 * LANES), lambda b, h, pg, bk: (h, 0, 0))]
                 + [page_spec(j) for j in range(n_src)] * 2,
        out_specs=vec_spec,
    )
    r4 = lambda a: a.reshape(B, H, 1, HEAD_DIM)
    out = pl.pallas_call(
        functools.partial(_decode_attend_kernel, past_len=past_len),
        grid_spec=grid_spec,
        out_shape=jax.ShapeDtypeStruct((B, H, 1, HEAD_DIM), F32),
        compiler_params=_params(("parallel", "arbitrary"), 4 * n_src * PAGE_SIZE * HEAD_DIM * 4),
        name="moba_decode_attend",
    )(pages.reshape(-1), blocks.reshape(-1), r4(q), r4(k_new), r4(v_new), tab,
      *([ck] * n_src), *([cv] * n_src))
    return out.reshape(B, D_MODEL)


def _peer_select_kernel(qq_ref, k1_ref, k2_ref, a_ref, b_ref, g_ref):
    tq = qq_ref.shape[0]
    k1 = k1_ref[...]
    k2 = k2_ref[...]
    a_rows, b_rows, g_rows = [], [], []
    for h in range(PEER_HEADS):
        q1 = qq_ref[:, (2 * h) * PEER_HALF:(2 * h + 1) * PEER_HALF]
        q2 = qq_ref[:, (2 * h + 1) * PEER_HALF:(2 * h + 2) * PEER_HALF]
        v1, i1 = _topk_rows(_dot_nt_x3(k1, q1), PEER_TOPK)
        v2, i2 = _topk_rows(_dot_nt_x3(k2, q2), PEER_TOPK)
        cand, pos_ids = [], []
        for r in range(PEER_TOPK):
            cnt = PEER_TOPK // (r + 1)
            cand.append(v1[r:r + 1] + v2[:cnt])
            pos_ids.append((lax.broadcasted_iota(I32, (cnt, tq), 0) + r * PEER_TOPK).astype(F32))
        sc, pos = _topk_rows(jnp.concatenate(cand, axis=0), PEER_TOPK, jnp.concatenate(pos_ids, axis=0))
        pos = pos.astype(I32)
        r_k = pos >> 4
        c_k = pos & (PEER_TOPK - 1)
        a_k = jnp.zeros_like(sc)
        b_k = jnp.zeros_like(sc)
        for r in range(PEER_TOPK):
            a_k = jnp.where(r_k == r, i1[r:r + 1], a_k)
            b_k = jnp.where(c_k == r, i2[r:r + 1], b_k)
        e = jnp.exp(sc - sc[0:1])
        g_rows.append(e / jnp.sum(e, axis=0, keepdims=True))
        a_rows.append(a_k)
        b_rows.append(b_k)
    a_ref[...] = jnp.concatenate(a_rows, axis=0).T
    b_ref[...] = jnp.concatenate(b_rows, axis=0).T
    g_ref[...] = jnp.concatenate(g_rows, axis=0).T


def _peer_select(qq, keys1, keys2):
    T = qq.shape[0]
    tq = min(T, 256)
    assert T % tq == 0
    spec = pl.BlockSpec((tq, PEER_PAIRS), lambda i: (i, 0))
    key_spec = pl.BlockSpec((PEER_N_KEYS, PEER_HALF), lambda i: (0, 0))
    shape = jax.ShapeDtypeStruct((T, PEER_PAIRS), F32)
    return pl.pallas_call(
        _peer_select_kernel,
        grid=(T // tq,),
        in_specs=[pl.BlockSpec((tq, 2 * PEER_HEADS * PEER_HALF), lambda i: (i, 0)), key_spec, key_spec],
        out_specs=[spec, spec, spec],
        out_shape=[shape, shape, shape],
        compiler_params=_params(("parallel",), 4 * tq * 2048 * 4),
        name="peer_select",
    )(qq, keys1, keys2)


def _gelu_exact(x):
    return 0.5 * x * (1.0 + lax.erf(x * (2.0 ** -0.5)))


PEER_TOKEN_TILE = 512
PEER_BUILD_ROWS = 16
PEER_KEY1_PER_TILE = 8


def _peer_mix_kernel(a_ref, b_ref, g_ref, h_ref, ut_ref, v_ref, x_ref, y_ref, gmat_scr, acc_scr, stage_scr, *, groups):
    j = pl.program_id(1)
    tq = h_ref.shape[0]
    rows = PEER_BUILD_ROWS

    @pl.when(j == 0)
    def _():
        acc_scr[...] = jnp.zeros_like(acc_scr)
        key_b = lax.broadcasted_iota(I32, (PEER_N_KEYS, PEER_PAIRS), 0).astype(F32).astype(BF16)
        one_b = jnp.ones((PEER_N_KEYS, PEER_PAIRS), BF16)
        zero_b = jnp.zeros((PEER_N_KEYS, PEER_PAIRS), BF16)

        n_row_groups = tq // rows

        def token_mats(tg, slot):
            t0 = pl.multiple_of(jnp.minimum(tg, n_row_groups - 1) * rows, rows)
            a8 = a_ref[pl.ds(t0, rows), :]
            b8 = b_ref[pl.ds(t0, rows), :]
            g8 = g_ref[pl.ds(t0, rows), :].astype(BF16)
            for r in range(rows):
                a_hot = jnp.where(key_b == a8[r:r + 1].astype(BF16), one_b, zero_b)
                b_hot = jnp.where(key_b == b8[r:r + 1].astype(BF16), g8[r:r + 1], zero_b)
                stage_scr[slot, r] = _dot_nt(a_hot, b_hot).astype(BF16)

        def file_mats(tg, slot):
            t0 = pl.multiple_of(tg * rows, rows)
            gmat_scr[:, pl.ds(t0, rows), :] = jnp.swapaxes(stage_scr[slot], 0, 1)

        token_mats(0, 0)

        def build(p, carry):
            token_mats(2 * p + 1, 1)
            file_mats(2 * p, 0)
            token_mats(2 * p + 2, 0)
            file_mats(2 * p + 1, 1)
            return carry

        lax.fori_loop(0, n_row_groups // 2, build, 0)

    act = _gelu_exact(jnp.dot(h_ref[...], ut_ref[...], preferred_element_type=F32))
    w = jnp.concatenate([gmat_scr[j * groups + a] for a in range(groups)], axis=-1)
    z = (act * w.astype(F32)).astype(BF16)
    acc_scr[...] += jnp.dot(z, v_ref[...], preferred_element_type=F32)

    @pl.when(j == pl.num_programs(1) - 1)
    def _():
        y_ref[...] = x_ref[...] + acc_scr[...]


def _peer_mix(a_idx, b_idx, g, h_bf16, ut_bf16, v_bf16, x):
    T, D = x.shape
    E = v_bf16.shape[0]
    tq = min(T, PEER_TOKEN_TILE)
    groups = PEER_KEY1_PER_TILE
    te = groups * PEER_N_KEYS
    assert T % tq == 0 and tq % (2 * PEER_BUILD_ROWS) == 0 and E % te == 0 and E == PEER_N_KEYS * PEER_N_KEYS
    pair_spec = pl.BlockSpec((tq, PEER_PAIRS), lambda i, j: (i, 0))
    tok_spec = pl.BlockSpec((tq, D), lambda i, j: (i, 0))
    est = tq * PEER_N_KEYS * PEER_N_KEYS * 2 + 2 * (2 * te * D * 2 + 3 * tq * D * 4) + 4 * tq * te * 4
    return pl.pallas_call(
        functools.partial(_peer_mix_kernel, groups=groups),
        grid=(T // tq, E // te),
        in_specs=[
            pair_spec, pair_spec, pair_spec, tok_spec,
            pl.BlockSpec((D, te), lambda i, j: (0, j)),
            pl.BlockSpec((te, D), lambda i, j: (j, 0)),
            tok_spec,
        ],
        out_specs=tok_spec,
        out_shape=jax.ShapeDtypeStruct((T, D), F32),
        scratch_shapes=[
            pltpu.VMEM((PEER_N_KEYS, tq, PEER_N_KEYS), BF16),
            pltpu.VMEM((tq, D), F32),
            pltpu.VMEM((2, PEER_BUILD_ROWS, PEER_N_KEYS, PEER_N_KEYS), BF16),
        ],
        compiler_params=_params(("parallel", "arbitrary"), est),
        name="peer_mix",
    )(a_idx, b_idx, g, h_bf16, ut_bf16, v_bf16, x)


def _peer(x, norm_gain, w_q, keys1, keys2, ut_bf16, v_bf16):
    qq, h_bf16 = _norm_proj(x, norm_gain, w_q, emit_h=True)
    a_idx, b_idx, g = _peer_select(qq, keys1, keys2)
    return _peer_mix(a_idx, b_idx, g, h_bf16, ut_bf16, v_bf16, x)


def kernel(x_prompt, x_sample, state_hgrn, cache_k, cache_v, page_table, norm_mix, norm_ffn, hgrn_w_in, hgrn_lb,
           hgrn_o_norm, hgrn_w_out, moba_w_in, moba_q_norm, moba_k_norm, moba_w_out, rel_bias, peer_w_q,
           peer_keys1, peer_keys2, peer_u, peer_v):
    depth = norm_mix.shape[0]
    Bp, Tp, D = x_prompt.shape
    Bs, Ts, _ = x_sample.shape
    assert Bp == 1 and Ts == 1 and D == D_MODEL
    H = N_HEADS
    lbs = jnp.cumsum(jax.nn.softmax(hgrn_lb.astype(F32), axis=0), axis=0)
    xp = x_prompt.reshape(Tp, D)
    xs = x_sample.reshape(Bs, D)
    st_p, st_s, kp_rows, vp_rows, ks_rows, vs_rows = [], [], [], [], [], []
    for layer in range(depth):
        if layer % 2 == 0:
            a = layer // 2
            w_in = hgrn_w_in[a].astype(BF16)
            w_out = hgrn_w_out[a].astype(BF16)
            proj_p, = _norm_proj(xp, norm_mix[layer], w_in)
            proj_s, = _norm_proj(xs, norm_mix[layer], w_in)
            o_p, s_p = _hgrn_prompt(proj_p, lbs[a])
            o_s, s_s = _hgrn_step(proj_s, lbs[a], state_hgrn[a])
            xp = _out_proj(o_p, w_out, xp, gate_src=proj_p, gate_col=3, gain=hgrn_o_norm[a])
            xs = _out_proj(o_s, w_out, xs, gate_src=proj_s, gate_col=3, gain=hgrn_o_norm[a])
            st_p.append(s_p[None])
            st_s.append(s_s)
        else:
            b = layer // 2
            w_in = moba_w_in[b].astype(BF16)
            w_out = moba_w_out[b].astype(BF16)
            head_gain = jnp.concatenate([jnp.tile(moba_q_norm[b], H), jnp.tile(moba_k_norm[b], H), jnp.ones((D,), F32)])
            q_p, k_p, v_p, k_p16, vt_p16, kmean_p = _moba_qkv(xp, norm_mix[layer], w_in, moba_q_norm[b], moba_k_norm[b])
            qkv_s, = _norm_proj(xs, norm_mix[layer], w_in, head_gain=head_gain, n_head_norm_tiles=2)
            att_p = _moba_prompt(q_p, k_p16, vt_p16, kmean_p, rel_bias)
            q_s = qkv_s[:, :D].reshape(Bs, H, HEAD_DIM)
            k_s = qkv_s[:, D:2 * D].reshape(Bs, H, HEAD_DIM)
            v_s = qkv_s[:, 2 * D:].reshape(Bs, H, HEAD_DIM)
            att_s = _moba_decode(q_s, k_s, v_s, cache_k, cache_v, b, page_table, rel_bias)
            xp = _out_proj(att_p, w_out, xp)
            xs = _out_proj(att_s, w_out, xs)
            kp_rows.append(k_p.reshape(Bp, Tp, H, HEAD_DIM))
            vp_rows.append(v_p.reshape(Bp, Tp, H, HEAD_DIM))
            ks_rows.append(k_s.reshape(Bs, Ts, H, HEAD_DIM))
            vs_rows.append(v_s.reshape(Bs, Ts, H, HEAD_DIM))
        w_q = peer_w_q[layer].astype(BF16)
        u16 = peer_u[layer].astype(BF16).T
        v16 = peer_v[layer].astype(BF16)
        xp = _peer(xp, norm_ffn[layer], w_q, peer_keys1[layer], peer_keys2[layer], u16, v16)
        xs = _peer(xs, norm_ffn[layer], w_q, peer_keys1[layer], peer_keys2[layer], u16, v16)
    return (xp.reshape(Bp, Tp, D), xs.reshape(Bs, Ts, D), jnp.stack(st_p), jnp.stack(st_s),
            jnp.stack(kp_rows), jnp.stack(vp_rows), jnp.stack(ks_rows), jnp.stack(vs_rows))
```

```python
import functools
import math

import numpy as np
import jax
import jax.numpy as jnp
from jax import lax
from jax.experimental import pallas as pl
from jax.experimental.pallas import tpu as pltpu

F32 = jnp.float32
BF16 = jnp.bfloat16
I32 = jnp.int32

D_MODEL = 1024
HEAD_DIM = 128
N_HEADS = D_MODEL // HEAD_DIM
PAGE_SIZE = 128
HGRN_CHUNK = 64
MOBA_BLOCK = 256
MOBA_TOPK = 3
REL_BUCKETS = 32
REL_MAX_DIST = 128
PEER_HEADS = 8
PEER_N_KEYS = 128
PEER_HALF = 128
PEER_TOPK = 16
PEER_PAIRS = PEER_HEADS * PEER_TOPK
RMS_EPS = 1e-6

LANES = 128
SUBLANES = 8
V7X_VMEM_BYTES = 64 * 1024 * 1024
NEG_BIG = -1e30
LOG2_E = 1.4426950408889634


def _vmem_limit(estimate_bytes):
    return int(min(max(2 * estimate_bytes, 16 * 1024 * 1024), V7X_VMEM_BYTES - 8 * 1024 * 1024))


def _params(sem, estimate_bytes):
    return pltpu.CompilerParams(dimension_semantics=sem, vmem_limit_bytes=_vmem_limit(estimate_bytes))


def _split2(x):
    hi = x.astype(BF16)
    lo = (x - hi.astype(F32)).astype(BF16)
    return hi, lo


def _split3(x):
    hi = x.astype(BF16)
    r = x - hi.astype(F32)
    mid = r.astype(BF16)
    lo = (r - mid.astype(F32)).astype(BF16)
    return hi, mid, lo


_NT = (((1,), (1,)), ((), ()))
_TN = (((0,), (0,)), ((), ()))


def _dot_nt(a, b):
    return lax.dot_general(a, b, _NT, preferred_element_type=F32)


def _dot_tn(a, b):
    return lax.dot_general(a, b, _TN, preferred_element_type=F32)


def _dot_nt_x3(a, b):
    ah, al = _split2(a)
    bh, bl = _split2(b)
    return _dot_nt(ah, bh) + _dot_nt(ah, bl) + _dot_nt(al, bh)


def _sigmoid(x):
    return 1.0 / (1.0 + jnp.exp(-x))


def _head_rms(y, gain):
    outs = []
    for h in range(y.shape[1] // HEAD_DIM):
        yh = y[:, h * HEAD_DIM:(h + 1) * HEAD_DIM]
        ms = jnp.mean(yh * yh, axis=-1, keepdims=True)
        outs.append(yh * lax.rsqrt(ms + RMS_EPS))
    return jnp.concatenate(outs, axis=-1) * gain


def _topk_rows(s, k, ids=None):
    if ids is None:
        ids = lax.broadcasted_iota(I32, s.shape, 0).astype(F32)
    vals, idxs = [], []
    cur = s
    for _ in range(k):
        m = jnp.max(cur, axis=0, keepdims=True)
        idx = jnp.min(jnp.where(cur == m, ids, jnp.inf), axis=0, keepdims=True)
        vals.append(m)
        idxs.append(idx)
        cur = jnp.where(ids == idx, -jnp.inf, cur)
    return jnp.concatenate(vals, axis=0), jnp.concatenate(idxs, axis=0)


def _norm_cast(x_ref, g_ref):
    x = x_ref[...]
    ms = jnp.mean(x * x, axis=-1, keepdims=True)
    return (x * lax.rsqrt(ms + RMS_EPS) * g_ref[...]).astype(BF16)


def _norm_proj_kernel(x_ref, g_ref, w_ref, hg_ref, y_ref, *h_out, n_head_norm_tiles):
    h = _norm_cast(x_ref, g_ref)
    y = jnp.dot(h, w_ref[...], preferred_element_type=F32)
    if n_head_norm_tiles:
        y = jnp.where(pl.program_id(1) < n_head_norm_tiles, _head_rms(y, hg_ref[...]), y)
    y_ref[...] = y
    if h_out:
        h_out[0][...] = h


def _norm_proj(x, gain, w_bf16, *, head_gain=None, n_head_norm_tiles=0, emit_h=False, tn=1024):
    T, D = x.shape
    N = w_bf16.shape[1]
    tm = min(T, 1024)
    assert T % tm == 0 and N % tn == 0
    if head_gain is None:
        head_gain = jnp.ones((N,), F32)
    out_shape = [jax.ShapeDtypeStruct((T, N), F32)]
    out_specs = [pl.BlockSpec((tm, tn), lambda i, j: (i, j))]
    if emit_h:
        out_shape.append(jax.ShapeDtypeStruct((T, D), BF16))
        out_specs.append(pl.BlockSpec((tm, D), lambda i, j: (i, 0)))
    est = 2 * (tm * D * 4 + D * tn * 2 + tm * tn * 4 + tm * D * 2) + tm * tn * 8
    return pl.pallas_call(
        functools.partial(_norm_proj_kernel, n_head_norm_tiles=n_head_norm_tiles),
        grid=(T // tm, N // tn),
        in_specs=[
            pl.BlockSpec((tm, D), lambda i, j: (i, 0)),
            pl.BlockSpec((1, D), lambda i, j: (0, 0)),
            pl.BlockSpec((D, tn), lambda i, j: (0, j)),
            pl.BlockSpec((1, tn), lambda i, j: (0, j)),
        ],
        out_specs=out_specs,
        out_shape=out_shape,
        compiler_params=_params(("parallel", "arbitrary"), est),
        name="norm_proj",
    )(x, gain.reshape(1, D), w_bf16, head_gain.reshape(1, N))


VT_ONES_ROWS = 16


def _moba_qkv_kernel(x_ref, g_ref, w_ref, hg_ref, wvt_ref, q_ref, k_ref, v_ref, kb_ref, vt_ref, kmean_ref):
    j = pl.program_id(1)
    tm = x_ref.shape[0]
    h = _norm_cast(x_ref, g_ref)
    y = jnp.dot(h, w_ref[...], preferred_element_type=F32)

    @pl.when(j == 0)
    def _():
        q_ref[...] = _head_rms(y, hg_ref[...])

    @pl.when(j == 1)
    def _():
        k = _head_rms(y, hg_ref[...])
        k_ref[...] = k
        kb_ref[...] = k.astype(BF16)
        for b in range(tm // MOBA_BLOCK):
            kmean_ref[b] = jnp.mean(k[b * MOBA_BLOCK:(b + 1) * MOBA_BLOCK], axis=0, keepdims=True)

    @pl.when(j == 2)
    def _():
        v_ref[...] = y
        vt = _dot_nt(wvt_ref[...], h).astype(BF16)
        vt_ref[:, :HEAD_DIM, :] = vt.reshape(N_HEADS, HEAD_DIM, tm)
        vt_ref[:, HEAD_DIM:, :] = jnp.ones((N_HEADS, VT_ONES_ROWS, tm), BF16)


def _moba_qkv(x, gain, w_bf16, q_gain, k_gain):
    T, D = x.shape
    H = N_HEADS
    tm = min(T, 1024)
    assert T % tm == 0 and tm % MOBA_BLOCK == 0 and w_bf16.shape == (D, 3 * D)
    nb = T // MOBA_BLOCK
    head_gain = jnp.concatenate([jnp.tile(q_gain, H), jnp.tile(k_gain, H), jnp.ones((D,), F32)]).reshape(1, 3 * D)
    tok = pl.BlockSpec((tm, D), lambda i, j: (i, 0))
    rows = HEAD_DIM + VT_ONES_ROWS
    est = 2 * (tm * D * (4 + 3 * 4 + 2) + 2 * D * D * 2 + H * rows * tm * 2) + 2 * tm * D * 4
    q, k, v, kb, vt, kmean = pl.pallas_call(
        _moba_qkv_kernel,
        grid=(T // tm, 3),
        in_specs=[
            tok,
            pl.BlockSpec((1, D), lambda i, j: (0, 0)),
            pl.BlockSpec((D, D), lambda i, j: (0, j)),
            pl.BlockSpec((1, D), lambda i, j: (0, j)),
            pl.BlockSpec((D, D), lambda i, j: (0, 0)),
        ],
        out_specs=[tok, tok, tok, tok,
                   pl.BlockSpec((H, rows, tm), lambda i, j: (0, 0, i)),
                   pl.BlockSpec((tm // MOBA_BLOCK, 1, D), lambda i, j: (i, 0, 0))],
        out_shape=[jax.ShapeDtypeStruct((T, D), F32)] * 3 + [
            jax.ShapeDtypeStruct((T, D), BF16),
            jax.ShapeDtypeStruct((H, rows, T), BF16),
            jax.ShapeDtypeStruct((nb, 1, D), F32)],
        compiler_params=_params(("parallel", "arbitrary"), est),
        name="moba_qkv",
    )(x, gain.reshape(1, D), w_bf16, head_gain, w_bf16[:, 2 * D:].T)
    return q, k, v, kb, vt, kmean.reshape(nb, D)


def _out_proj_kernel(o_ref, gate_ref, gain_ref, w_ref, x_ref, y_ref, *, gated):
    o = o_ref[...]
    if gated:
        g = gate_ref[...]
        o = _head_rms(o, gain_ref[...]) * (g * _sigmoid(g))
    y_ref[...] = x_ref[...] + jnp.dot(o.astype(BF16), w_ref[...], preferred_element_type=F32)


def _out_proj(o, w_bf16, x, *, gate_src=None, gate_col=0, gain=None):
    T, D = o.shape
    tm = min(T, 512)
    assert T % tm == 0
    gated = gate_src is not None
    if not gated:
        gate_src, gain = o, jnp.ones((D,), F32)
    est = 2 * (4 * tm * D * 4 + D * D * 2)
    return pl.pallas_call(
        functools.partial(_out_proj_kernel, gated=gated),
        grid=(T // tm,),
        in_specs=[
            pl.BlockSpec((tm, D), lambda i: (i, 0)),
            pl.BlockSpec((tm, D), lambda i: (i, gate_col)),
            pl.BlockSpec((1, D), lambda i: (0, 0)),
            pl.BlockSpec((D, D), lambda i: (0, 0)),
            pl.BlockSpec((tm, D), lambda i: (i, 0)),
        ],
        out_specs=pl.BlockSpec((tm, D), lambda i: (i, 0)),
        out_shape=jax.ShapeDtypeStruct((T, D), F32),
        compiler_params=_params(("parallel",), est),
        name="out_proj",
    )(o, gate_src, gain.reshape(1, D), w_bf16, x)


HGRN_SAFE_DECAY = 80.0


def _hgrn_prompt_kernel(q_ref, fl_ref, v_ref, lb_ref, o_ref, s_out_ref, st_scr, b_scr, k_scr):
    c = pl.program_id(0)
    C = q_ref.shape[0]

    @pl.when(c == 0)
    def _():
        st_scr[...] = jnp.zeros_like(st_scr)

    lb = lb_ref[...]
    f = lb + (1.0 - lb) * _sigmoid(fl_ref[...])
    logf = jnp.log(f)
    q = q_ref[...]
    qq = q * _sigmoid(q)
    tri = (lax.broadcasted_iota(I32, (C, C), 0) >= lax.broadcasted_iota(I32, (C, C), 1)).astype(BF16)
    p0, p1, p2 = _split3(logf)
    b = (jnp.dot(tri, p0, preferred_element_type=F32) + jnp.dot(tri, p1, preferred_element_type=F32)
         + jnp.dot(tri, p2, preferred_element_type=F32))
    b_scr[...] = b
    k_scr[...] = 1.0 - f
    row = lax.broadcasted_iota(I32, (C, HEAD_DIM), 0)
    col = lax.broadcasted_iota(I32, (C, C), 1)
    causal = lax.broadcasted_iota(I32, (C, C), 0) >= col

    def head_step(h, hs, a_mat):
        bh = b_scr[:, hs]
        vh = v_ref[:, hs].astype(BF16)
        bl = bh[C - 1:C, :]
        qh = q_ref[:, hs]
        qh = qh * _sigmoid(qh)
        st = st_scr[h]
        inter = _dot_nt((qh * jnp.exp(bh)).astype(BF16), st.astype(BF16))
        intra = jnp.dot(a_mat.astype(BF16), vh, preferred_element_type=F32)
        o_ref[:, hs] = inter + intra
        k_dec = (k_scr[:, hs] * jnp.exp(bl - bh)).astype(BF16)
        st_scr[h] = st * jnp.exp(bl) + _dot_tn(vh, k_dec)

    safe = jnp.min(b[C - 1:C, :]) >= -HGRN_SAFE_DECAY

    @pl.when(safe)
    def _():
        for h in range(N_HEADS):
            hs = slice(h * HEAD_DIM, (h + 1) * HEAD_DIM)
            bh = b[:, hs]
            half = 0.5 * bh[C - 1:C, :]
            q_dec = (qq[:, hs] * jnp.exp(bh - half)).astype(BF16)
            k_inc = (k_scr[:, hs] * jnp.exp(half - bh)).astype(BF16)
            head_step(h, hs, jnp.where(causal, _dot_nt(q_dec, k_inc), 0.0))

    @pl.when(jnp.logical_not(safe))
    def _():
        for h in range(N_HEADS):
            hs = slice(h * HEAD_DIM, (h + 1) * HEAD_DIM)
            bh = b[:, hs]
            qh = qq[:, hs]

            def pair_cols(s8, a_mat):
                start = pl.multiple_of(s8 * SUBLANES, SUBLANES)
                b8 = b_scr[pl.ds(start, SUBLANES), hs]
                k8 = k_scr[pl.ds(start, SUBLANES), hs]
                for r in range(SUBLANES):
                    s = start + r
                    d = jnp.exp(jnp.where(row >= s, bh - b8[r:r + 1], -jnp.inf))
                    a_col = jnp.sum(qh * k8[r:r + 1] * d, axis=-1, keepdims=True)
                    a_mat = jnp.where(col == s, a_col, a_mat)
                return a_mat

            head_step(h, hs, lax.fori_loop(0, C // SUBLANES, pair_cols, jnp.zeros((C, C), F32)))

    @pl.when(c == pl.num_programs(0) - 1)
    def _():
        for h in range(N_HEADS):
            s_out_ref[h] = st_scr[h].T


def _hgrn_prompt(proj, lb):
    T = proj.shape[0]
    C = min(HGRN_CHUNK, T)
    assert T % C == 0 and C % SUBLANES == 0
    D = D_MODEL
    est = 2 * 4 * C * D * 4 + 3 * N_HEADS * HEAD_DIM * HEAD_DIM * 4 + 2 * C * D * 4
    return pl.pallas_call(
        _hgrn_prompt_kernel,
        grid=(T // C,),
        in_specs=[
            pl.BlockSpec((C, D), lambda c: (c, 0)),
            pl.BlockSpec((C, D), lambda c: (c, 1)),
            pl.BlockSpec((C, D), lambda c: (c, 2)),
            pl.BlockSpec((1, D), lambda c: (0, 0)),
        ],
        out_specs=[
            pl.BlockSpec((C, D), lambda c: (c, 0)),
            pl.BlockSpec((N_HEADS, HEAD_DIM, HEAD_DIM), lambda c: (0, 0, 0)),
        ],
        out_shape=[
            jax.ShapeDtypeStruct((T, D), F32),
            jax.ShapeDtypeStruct((N_HEADS, HEAD_DIM, HEAD_DIM), F32),
        ],
        scratch_shapes=[
            pltpu.VMEM((N_HEADS, HEAD_DIM, HEAD_DIM), F32),
            pltpu.VMEM((C, D), F32),
            pltpu.VMEM((C, D), F32),
        ],
        compiler_params=_params(("arbitrary",), est),
        name="hgrn_prompt",
    )(proj, proj, proj, lb.reshape(1, D))


def _hgrn_step_kernel(q_ref, fl_ref, lb_ref, v_ref, s_ref, o_ref, s_out_ref):
    lb = lb_ref[...]
    f = lb + (1.0 - lb) * _sigmoid(fl_ref[...])
    q = q_ref[...]
    qq = q * _sigmoid(q)
    s_new = s_ref[...] * f + (1.0 - f) * v_ref[...]
    s_out_ref[...] = s_new
    o_ref[...] = jnp.sum(s_new * qq, axis=1, keepdims=True)


def _hgrn_step(proj, lb, state):
    B = proj.shape[0]
    H, DK = N_HEADS, HEAD_DIM
    q = proj[:, :D_MODEL].reshape(B, H, DK, 1)
    fl = proj[:, D_MODEL:2 * D_MODEL].reshape(B, H, DK, 1)
    v = proj[:, 2 * D_MODEL:3 * D_MODEL].reshape(B, H, 1, DK)
    col_spec = pl.BlockSpec((None, H, DK, 1), lambda b: (b, 0, 0, 0))
    row_spec = pl.BlockSpec((None, H, 1, DK), lambda b: (b, 0, 0, 0))
    est = 2 * (3 * H * DK * LANES * 4 + 2 * H * DK * DK * 4)
    o, s_new = pl.pallas_call(
        _hgrn_step_kernel,
        grid=(B,),
        in_specs=[
            col_spec, col_spec,
            pl.BlockSpec((H, DK, 1), lambda b: (0, 0, 0)),
            row_spec,
            pl.BlockSpec((None, H, DK, DK), lambda b: (b, 0, 0, 0)),
        ],
        out_specs=[row_spec, pl.BlockSpec((None, H, DK, DK), lambda b: (b, 0, 0, 0))],
        out_shape=[jax.ShapeDtypeStruct((B, H, 1, DK), F32), jax.ShapeDtypeStruct(state.shape, F32)],
        compiler_params=_params(("parallel",), est),
        name="hgrn_step",
    )(q, fl, lb.reshape(H, DK, 1), v, state)
    return o.reshape(B, D_MODEL), s_new


def _rel_bucket_table():
    d = np.arange(REL_MAX_DIST + 1)
    max_exact = REL_BUCKETS // 2
    large = max_exact + (np.log(np.maximum(d, max_exact).astype(np.float32) / max_exact)
                         / math.log(REL_MAX_DIST / max_exact) * (REL_BUCKETS - max_exact)).astype(np.int32)
    large = np.minimum(large, REL_BUCKETS - 1)
    tab = np.where(d < max_exact, d, large).astype(np.int32)
    assert tab[REL_MAX_DIST] == REL_BUCKETS - 1
    return tab


MOBA_FAR_GROUP = 4


def _moba_prompt_kernel(q_ref, kmean_ref, k_ref, vt_ref, bias_ref, far_ref, o_ref, sel_scr, m_scr, acc_scr,
                        s_scr, *, group):
    c0 = 2 * pl.program_id(1)
    tq = q_ref.shape[0]
    nb = kmean_ref.shape[0]
    q = q_ref[...]
    qs = (q * (HEAD_DIM ** -0.5 * LOG2_E)).astype(BF16)

    gate = _dot_nt_x3(kmean_ref[...], q)
    blk = lax.broadcasted_iota(I32, (nb, tq), 0)
    past = blk < c0 + (lax.broadcasted_iota(I32, (nb, tq), 1) >= MOBA_BLOCK).astype(I32)
    cur = jnp.where(past, gate, -jnp.inf)
    sel = jnp.zeros((nb, tq), F32)
    for _ in range(MOBA_TOPK):
        m = jnp.max(cur, axis=0, keepdims=True)
        idx = jnp.min(jnp.where((cur == m) & past, blk, nb), axis=0, keepdims=True)
        hit = (blk == idx) & (cur > -jnp.inf)
        sel = jnp.where(hit, 1.0, sel)
        cur = jnp.where(blk == idx, -jnp.inf, cur)
    pad = sel_scr.shape[0] - nb
    sel_scr[...] = jnp.concatenate([sel, jnp.zeros((pad, tq), F32)], axis=0) if pad else sel
    sub = lax.broadcasted_iota(I32, (SUBLANES, tq), 0)

    def allowed(n):
        start = pl.multiple_of((n // SUBLANES) * SUBLANES, SUBLANES)
        rows = sel_scr[pl.ds(start, SUBLANES), :]
        return jnp.max(jnp.where(sub == n - start, rows, 0.0), axis=0, keepdims=True) > 0.0

    m_scr[...] = jnp.full_like(m_scr, NEG_BIG)
    acc_scr[...] = jnp.zeros_like(acc_scr)

    far = far_ref[0:1, 0:1]

    n_groups_total = nb // group

    def group_start(ng):
        return pl.multiple_of(jnp.minimum(ng, n_groups_total - 1) * (group * MOBA_BLOCK), group * MOBA_BLOCK)

    def far_scores(ng, slot):
        s_scr[slot] = _dot_nt(k_ref[pl.ds(group_start(ng), group * MOBA_BLOCK), :], qs)

    def far_update(ng, slot):
        n0 = ng * group
        start = group_start(ng)
        parts = [s_scr[slot, g * MOBA_BLOCK:(g + 1) * MOBA_BLOCK, :] for g in range(group)]
        allow = [allowed(jnp.minimum(n0 + g, nb - 1)) & (n0 + g < c0 - 1) for g in range(group)]
        mx = jnp.where(allow[0], jnp.max(parts[0], axis=0, keepdims=True), NEG_BIG)
        for g in range(1, group):
            mx = jnp.maximum(mx, jnp.where(allow[g], jnp.max(parts[g], axis=0, keepdims=True), NEG_BIG))
        m_old = m_scr[...]
        m_new = jnp.maximum(m_old, mx + far)
        alpha = jnp.exp2(m_old - m_new)
        shift = m_new - far
        acc = alpha * acc_scr[...]
        for g in range(group):
            p = jnp.exp2(parts[g] - shift)
            start_g = pl.multiple_of(start + g * MOBA_BLOCK, MOBA_BLOCK)
            pv = jnp.dot(vt_ref[:, pl.ds(start_g, MOBA_BLOCK)], p.astype(BF16), preferred_element_type=F32)
            acc = acc + jnp.where(allow[g], pv, 0.0)
        acc_scr[...] = acc
        m_scr[...] = m_new

    n_groups = (c0 + group - 2) // group
    far_scores(0, 0)

    def far_pair(p, carry):
        far_scores(2 * p + 1, 1)
        far_update(2 * p, 0)
        far_scores(2 * p + 2, 0)
        far_update(2 * p + 1, 1)
        return carry

    lax.fori_loop(0, (n_groups + 1) // 2, far_pair, 0)

    tile = (MOBA_BLOCK, tq)
    row = lax.broadcasted_iota(I32, tile, 0)
    lane = lax.broadcasted_iota(I32, tile, 1)
    first = lane < MOBA_BLOCK
    causal_first = first & (row <= lane)
    causal_second = jnp.logical_not(first) & (row <= lane - MOBA_BLOCK)
    bias_own, bias_prev = bias_ref[0], bias_ref[1]

    def near_pass(start_block, pieces):
        start = pl.multiple_of(start_block * MOBA_BLOCK, MOBA_BLOCK)
        s = _dot_nt(k_ref[pl.ds(start, len(pieces) * MOBA_BLOCK), :], qs)
        parts = [s[g * MOBA_BLOCK:(g + 1) * MOBA_BLOCK] + bias for g, (bias, _) in enumerate(pieces)]
        mx = None
        for part, (_, allow) in zip(parts, pieces):
            col_max = jnp.max(jnp.where(allow, part, NEG_BIG), axis=0, keepdims=True)
            mx = col_max if mx is None else jnp.maximum(mx, col_max)
        m_old = m_scr[...]
        m_new = jnp.maximum(m_old, mx)
        acc = jnp.exp2(m_old - m_new) * acc_scr[...]
        for g, (part, (_, allow)) in enumerate(zip(parts, pieces)):
            p = jnp.where(allow, jnp.exp2(part - m_new), 0.0).astype(BF16)
            start_g = pl.multiple_of(start + g * MOBA_BLOCK, MOBA_BLOCK)
            acc = acc + jnp.dot(vt_ref[:, pl.ds(start_g, MOBA_BLOCK)], p, preferred_element_type=F32)
        acc_scr[...] = acc
        m_scr[...] = m_new

    def own_pieces():
        return [(jnp.concatenate([bias_own, bias_prev], axis=1), causal_first | (jnp.logical_not(first) & allowed(c0))),
                (jnp.concatenate([bias_own, bias_own], axis=1), causal_second)]

    @pl.when(c0 == 0)
    def _():
        near_pass(0, own_pieces())

    @pl.when(c0 >= 2)
    def _():
        far_tile = jnp.broadcast_to(far, (MOBA_BLOCK, MOBA_BLOCK))
        before = (jnp.concatenate([bias_prev, far_tile], axis=1), jnp.broadcast_to(allowed(c0 - 1), tile))
        near_pass(c0 - 1, [before] + own_pieces())

    o_ref[...] = (acc_scr[:HEAD_DIM, :] / acc_scr[HEAD_DIM:HEAD_DIM + 1, :]).T


def _moba_prompt(q, k_bf16, vt_ones, kmean, rel_bias):
    T = q.shape[0]
    assert T % MOBA_BLOCK == 0
    nb = T // MOBA_BLOCK
    H = N_HEADS
    tab =rel_bias[jnp.asarray(_rel_bucket_table())] * LOG2_E
    n = MOBA_BLOCK
    e = np.arange(2 * n)
    e_signed = np.where(e < n, e, e - 2 * n)
    w_own = tab[np.clip(e_signed, 0, REL_MAX_DIST)]
    w_prev = tab[np.clip(n + e_signed, 0, REL_MAX_DIST)]

    def toeplitz(w):
        return jnp.tile(w, n)[:n * (2 * n - 1)].reshape(n, 2 * n - 1)[:, :n]

    near = jnp.stack([jax.vmap(toeplitz, in_axes=1)(w_own), jax.vmap(toeplitz, in_axes=1)(w_prev)], axis=1)
    far = jnp.broadcast_to(tab[REL_MAX_DIST][:, None, None], (H, 8, LANES))
    group = math.gcd(nb, MOBA_FAR_GROUP)
    assert nb % 2 == 0
    tq = 2 * MOBA_BLOCK
    est = 2 * (2 * T * HEAD_DIM * 2 + 2 * MOBA_BLOCK * MOBA_BLOCK * 4) + (24 + 4 * group) * MOBA_BLOCK * tq * 4
    return pl.pallas_call(
        functools.partial(_moba_prompt_kernel, group=group),
        grid=(H, nb // 2),
        in_specs=[
            pl.BlockSpec((tq, HEAD_DIM), lambda h, i: (i, h)),
            pl.BlockSpec((nb, HEAD_DIM), lambda h, i: (0, h)),
            pl.BlockSpec((T, HEAD_DIM), lambda h, i: (0, h)),
            pl.BlockSpec((None, HEAD_DIM + VT_ONES_ROWS, T), lambda h, i: (h, 0, 0)),
            pl.BlockSpec((None, 2, MOBA_BLOCK, MOBA_BLOCK), lambda h, i: (h, 0, 0, 0)),
            pl.BlockSpec((None, 8, LANES), lambda h, i: (h, 0, 0)),
        ],
        out_specs=pl.BlockSpec((tq, HEAD_DIM), lambda h, i: (i, h)),
        out_shape=jax.ShapeDtypeStruct((T, D_MODEL), F32),
        scratch_shapes=[
            pltpu.VMEM((-(-nb // SUBLANES) * SUBLANES, tq), F32),
            pltpu.VMEM((1, tq), F32),
            pltpu.VMEM((HEAD_DIM + VT_ONES_ROWS, tq), F32),
            pltpu.VMEM((2, group * MOBA_BLOCK, tq), F32),
        ],
        compiler_params=_params(("parallel", "arbitrary"), est),
        name="moba_prompt",
    )(q, kmean, k_bf16, vt_ones, near, far)


PAGES_PER_BLOCK = MOBA_BLOCK // PAGE_SIZE
MEAN_BLOCKS_PER_STEP = 4


def _paged_block_mean_kernel(pt_ref, *refs):
    page_refs, o_ref = refs[:-1], refs[-1]
    for n in range(len(page_refs) // PAGES_PER_BLOCK):
        tot = jnp.sum(page_refs[n * PAGES_PER_BLOCK][...], axis=0)
        for p in range(1, PAGES_PER_BLOCK):
            tot = tot + jnp.sum(page_refs[n * PAGES_PER_BLOCK + p][...], axis=0)
        o_ref[n] = tot * (1.0 / MOBA_BLOCK)


def _paged_block_mean(cache_k, layer, page_table):
    B, n_pages = page_table.shape
    assert n_pages % PAGES_PER_BLOCK == 0
    nb = n_pages // PAGES_PER_BLOCK
    bps = math.gcd(nb, MEAN_BLOCKS_PER_STEP)
    pps = bps * PAGES_PER_BLOCK
    page_shape = (None, None, PAGE_SIZE, N_HEADS, HEAD_DIM)

    def page_spec(p):
        return pl.BlockSpec(page_shape, lambda b, n, pt: (layer, pt[b * n_pages + n * pps + p], 0, 0, 0))

    grid_spec = pltpu.PrefetchScalarGridSpec(
        num_scalar_prefetch=1,
        grid=(B, nb // bps),
        in_specs=[page_spec(p) for p in range(pps)],
        out_specs=pl.BlockSpec((None, bps, N_HEADS, HEAD_DIM), lambda b, n, pt: (b, n, 0, 0)),
    )
    return pl.pallas_call(
        _paged_block_mean_kernel,
        grid_spec=grid_spec,
        out_shape=jax.ShapeDtypeStruct((B, nb, N_HEADS, HEAD_DIM), F32),
        compiler_params=_params(("parallel", "arbitrary"), 2 * pps * PAGE_SIZE * D_MODEL * 4),
        name="moba_paged_block_mean",
    )(page_table.reshape(-1), *([cache_k] * pps))


def _decode_gate_kernel(q_ref, kmean_ref, o_ref):
    nb = kmean_ref.shape[0]
    gate = jnp.sum(kmean_ref[...] * q_ref[...][None], axis=-1, keepdims=True)
    blk = lax.broadcasted_iota(I32, gate.shape, 0)
    cur = gate
    for r in range(MOBA_TOPK):
        m = jnp.max(cur, axis=0, keepdims=True)
        idx = jnp.min(jnp.where(cur == m, blk, nb), axis=0, keepdims=True)
        o_ref[r] = jnp.broadcast_to(idx[0], (N_HEADS, LANES))
        cur = jnp.where(blk == idx, -jnp.inf, cur)


def _decode_gate(q, kmean):
    B, nb = kmean.shape[:2]
    out = pl.pallas_call(
        _decode_gate_kernel,
        grid=(B,),
        in_specs=[
            pl.BlockSpec((None, N_HEADS, HEAD_DIM), lambda b: (b, 0, 0)),
            pl.BlockSpec((None, nb, N_HEADS, HEAD_DIM), lambda b: (b, 0, 0, 0)),
        ],
        out_specs=pl.BlockSpec((None, MOBA_TOPK, N_HEADS, LANES), lambda b: (b, 0, 0, 0)),
        out_shape=jax.ShapeDtypeStruct((B, MOBA_TOPK, N_HEADS, LANES), I32),
        compiler_params=_params(("parallel",), 4 * nb * N_HEADS * HEAD_DIM * 4),
        name="moba_decode_gate",
    )(q, kmean)
    return out[..., 0].transpose(0, 2, 1)


def _decode_attend_kernel(pages_ref, blocks_ref, q_ref, kn_ref, vn_ref, tab_ref, *refs, past_len):
    n_src = MOBA_TOPK * 2
    k_refs, v_refs, o_ref = refs[:n_src], refs[n_src:2 * n_src], refs[2 * n_src]
    b, h = pl.program_id(0), pl.program_id(1)
    head_rows = pl.ds(h, PAGE_SIZE, stride=N_HEADS)
    scale = HEAD_DIM ** -0.5
    q = q_ref[...]
    tab = tab_ref[...]
    row = lax.broadcasted_iota(I32, (PAGE_SIZE, 1), 0)
    lane = lax.broadcasted_iota(I32, (PAGE_SIZE, 2 * LANES), 1)
    logits = []
    for j in range(n_src):
        blk = blocks_ref[(b * N_HEADS + h) * MOBA_TOPK + j // 2]
        rel = past_len - (blk * MOBA_BLOCK + (j % 2) * PAGE_SIZE + row)
        bias = jnp.sum(jnp.where(lane == jnp.minimum(rel, REL_MAX_DIST), tab, 0.0), axis=-1, keepdims=True)
        logits.append(jnp.sum(k_refs[j][head_rows, :] * q, axis=-1, keepdims=True) * scale + bias)
    own = jnp.sum(kn_ref[...] * q, axis=-1, keepdims=True) * scale + tab[:, 0:1]
    m = own
    for s in logits:
        m = jnp.maximum(m, jnp.max(s, axis=0, keepdims=True))
    p_own = jnp.exp(own - m)
    l = p_own
    acc = p_own * vn_ref[...]
    for j in range(n_src):
        p = jnp.exp(logits[j] - m)
        l = l + jnp.sum(p, axis=0, keepdims=True)
        acc = acc + jnp.sum(p * v_refs[j][head_rows, :], axis=0, keepdims=True)
    o_ref[...] = acc / l


def _moba_decode(q, k_new, v_new, cache_k, cache_v, layer, page_table, rel_bias):
    B, n_pages = page_table.shape
    past_len = n_pages * PAGE_SIZE
    assert past_len % MOBA_BLOCK == 0 and past_len // MOBA_BLOCK >= MOBA_TOPK
    H = N_HEADS
    kmean = _paged_block_mean(cache_k, layer, page_table)
    blocks = _decode_gate(q, kmean)
    pages = jnp.take_along_axis(
        page_table[:, None, :], (blocks[..., None] * 2 + jnp.arange(2, dtype=I32)).reshape(B, H, -1), axis=2)
    n_src = MOBA_TOPK * 2
    tab = rel_bias[jnp.asarray(_rel_bucket_table())].T
    tab = jnp.pad(tab, ((0, 0), (0, 2 * LANES - tab.shape[1]))).reshape(H, 1, 2 * LANES)
    n_layers, n_pool = cache_k.shape[:2]
    ck = cache_k.reshape(n_layers, n_pool, PAGE_SIZE * H, HEAD_DIM)
    cv = cache_v.reshape(n_layers, n_pool, PAGE_SIZE * H, HEAD_DIM)

    def page_spec(j):
        return pl.BlockSpec((None, None, PAGE_SIZE * H, HEAD_DIM),
                            lambda b, h, pg, bk: (layer, pg[(b * H + h) * n_src + j], 0, 0))

    vec_spec = pl.BlockSpec((None, None, 1, HEAD_DIM), lambda b, h, pg, bk: (b, h, 0, 0))
    grid_spec = pltpu.PrefetchScalarGridSpec(
        num_scalar_prefetch=2,
        grid=(B, H),
        in_specs=[vec_spec, vec_spec, vec_spec,
                  pl.BlockSpec((None, 1, 2 * LANES), lambda b, h, pg, bk: (h, 0, 0))]
                 + [page_spec(j) for j in range(n_src)] * 2,
        out_specs=vec_spec,
    )
    r4 = lambda a: a.reshape(B, H, 1, HEAD_DIM)
    out = pl.pallas_call(
        functools.partial(_decode_attend_kernel, past_len=past_len),
        grid_spec=grid_spec,
        out_shape=jax.ShapeDtypeStruct((B, H, 1, HEAD_DIM), F32),
        compiler_params=_params(("parallel", "arbitrary"), 4 * n_src * PAGE_SIZE * HEAD_DIM * 4),
        name="moba_decode_attend",
    )(pages.reshape(-1), blocks.reshape(-1), r4(q), r4(k_new), r4(v_new), tab,
      *([ck] * n_src), *([cv] * n_src))
    return out.reshape(B, D_MODEL)


def _peer_select_kernel(qq_ref, k1_ref, k2_ref, a_ref, b_ref, g_ref):
    tq = qq_ref.shape[0]
    k1 = k1_ref[...]
    k2 = k2_ref[...]
    a_rows, b_rows, g_rows = [], [], []
    for h in range(PEER_HEADS):
        q1 = qq_ref[:, (2 * h) * PEER_HALF:(2 * h + 1) * PEER_HALF]
        q2 = qq_ref[:, (2 * h + 1) * PEER_HALF:(2 * h + 2) * PEER_HALF]
        v1, i1 = _topk_rows(_dot_nt_x3(k1, q1), PEER_TOPK)
        v2, i2 = _topk_rows(_dot_nt_x3(k2, q2), PEER_TOPK)
        cand, pos_ids = [], []
        for r in range(PEER_TOPK):
            cnt = PEER_TOPK // (r + 1)
            cand.append(v1[r:r + 1] + v2[:cnt])
            pos_ids.append((lax.broadcasted_iota(I32, (cnt, tq), 0) + r * PEER_TOPK).astype(F32))
        sc, pos = _topk_rows(jnp.concatenate(cand, axis=0), PEER_TOPK, jnp.concatenate(pos_ids, axis=0))
        pos = pos.astype(I32)
        r_k = pos >> 4
        c_k = pos & (PEER_TOPK - 1)
        a_k = jnp.zeros_like(sc)
        b_k = jnp.zeros_like(sc)
        for r in range(PEER_TOPK):
            a_k = jnp.where(r_k == r, i1[r:r + 1], a_k)
            b_k = jnp.where(c_k == r, i2[r:r + 1], b_k)
        e = jnp.exp(sc - sc[0:1])
        g_rows.append(e / jnp.sum(e, axis=0, keepdims=True))
        a_rows.append(a_k)
        b_rows.append(b_k)
    a_ref[...] = jnp.concatenate(a_rows, axis=0).T
    b_ref[...] = jnp.concatenate(b_rows, axis=0).T
    g_ref[...] = jnp.concatenate(g_rows, axis=0).T


def _peer_select(qq, keys1, keys2):
    T = qq.shape[0]
    tq = min(T, 256)
    assert T % tq == 0
    spec = pl.BlockSpec((tq, PEER_PAIRS), lambda i: (i, 0))
    key_spec = pl.BlockSpec((PEER_N_KEYS, PEER_HALF), lambda i: (0, 0))
    shape = jax.ShapeDtypeStruct((T, PEER_PAIRS), F32)
    return pl.pallas_call(
        _peer_select_kernel,
        grid=(T // tq,),
        in_specs=[pl.BlockSpec((tq, 2 * PEER_HEADS * PEER_HALF), lambda i: (i, 0)), key_spec, key_spec],
        out_specs=[spec, spec, spec],
        out_shape=[shape, shape, shape],
        compiler_params=_params(("parallel",), 4 * tq * 2048 * 4),
        name="peer_select",
    )(qq, keys1, keys2)


def _gelu_exact(x):
    return 0.5 * x * (1.0 + lax.erf(x * (2.0 ** -0.5)))


PEER_TOKEN_TILE = 512
PEER_BUILD_ROWS = 16
PEER_KEY1_PER_TILE = 8


def _peer_mix_kernel(a_ref, b_ref, g_ref, h_ref, ut_ref, v_ref, x_ref, y_ref, gmat_scr, acc_scr, stage_scr, *, groups):
    j = pl.program_id(1)
    tq = h_ref.shape[0]
    rows = PEER_BUILD_ROWS

    @pl.when(j == 0)
    def _():
        acc_scr[...] = jnp.zeros_like(acc_scr)
        key_b = lax.broadcasted_iota(I32, (PEER_N_KEYS, PEER_PAIRS), 0).astype(F32).astype(BF16)
        one_b = jnp.ones((PEER_N_KEYS, PEER_PAIRS), BF16)
        zero_b = jnp.zeros((PEER_N_KEYS, PEER_PAIRS), BF16)

        n_row_groups = tq // rows

        def token_mats(tg, slot):
            t0 = pl.multiple_of(jnp.minimum(tg, n_row_groups - 1) * rows, rows)
            a8 = a_ref[pl.ds(t0, rows), :]
            b8 = b_ref[pl.ds(t0, rows), :]
            g8 = g_ref[pl.ds(t0, rows), :].astype(BF16)
            for r in range(rows):
                a_hot = jnp.where(key_b == a8[r:r + 1].astype(BF16), one_b, zero_b)
                b_hot = jnp.where(key_b == b8[r:r + 1].astype(BF16), g8[r:r + 1], zero_b)
                stage_scr[slot, r] = _dot_nt(a_hot, b_hot).astype(BF16)

        def file_mats(tg, slot):
            t0 = pl.multiple_of(tg * rows, rows)
            gmat_scr[:, pl.ds(t0, rows), :] = jnp.swapaxes(stage_scr[slot], 0, 1)

        token_mats(0, 0)

        def build(p, carry):
            token_mats(2 * p + 1, 1)
            file_mats(2 * p, 0)
            token_mats(2 * p + 2, 0)
            file_mats(2 * p + 1, 1)
            return carry

        lax.fori_loop(0, n_row_groups // 2, build, 0)

    act = _gelu_exact(jnp.dot(h_ref[...], ut_ref[...], preferred_element_type=F32))
    w = jnp.concatenate([gmat_scr[j * groups + a] for a in range(groups)], axis=-1)
    z = (act * w.astype(F32)).astype(BF16)
    acc_scr[...] += jnp.dot(z, v_ref[...], preferred_element_type=F32)

    @pl.when(j == pl.num_programs(1) - 1)
    def _():
        y_ref[...] = x_ref[...] + acc_scr[...]


def _peer_mix(a_idx, b_idx, g, h_bf16, ut_bf16, v_bf16, x):
    T, D = x.shape
    E = v_bf16.shape[0]
    tq = min(T, PEER_TOKEN_TILE)
    groups = PEER_KEY1_PER_TILE
    te = groups * PEER_N_KEYS
    assert T % tq == 0 and tq % (2 * PEER_BUILD_ROWS) == 0 and E % te == 0 and E == PEER_N_KEYS * PEER_N_KEYS
    pair_spec = pl.BlockSpec((tq, PEER_PAIRS), lambda i, j: (i, 0))
    tok_spec = pl.BlockSpec((tq, D), lambda i, j: (i, 0))
    est = tq * PEER_N_KEYS * PEER_N_KEYS * 2 + 2 * (2 * te * D * 2 + 3 * tq * D * 4) + 4 * tq * te * 4
    return pl.pallas_call(
        functools.partial(_peer_mix_kernel, groups=groups),
        grid=(T // tq, E // te),
        in_specs=[
            pair_spec, pair_spec, pair_spec, tok_spec,
            pl.BlockSpec((D, te), lambda i, j: (0, j)),
            pl.BlockSpec((te, D), lambda i, j: (j, 0)),
            tok_spec,
        ],
        out_specs=tok_spec,
        out_shape=jax.ShapeDtypeStruct((T, D), F32),
        scratch_shapes=[
            pltpu.VMEM((PEER_N_KEYS, tq, PEER_N_KEYS), BF16),
            pltpu.VMEM((tq, D), F32),
            pltpu.VMEM((2, PEER_BUILD_ROWS, PEER_N_KEYS, PEER_N_KEYS), BF16),
        ],
        compiler_params=_params(("parallel", "arbitrary"), est),
        name="peer_mix",
    )(a_idx, b_idx, g, h_bf16, ut_bf16, v_bf16, x)


def _peer(x, norm_gain, w_q, keys1, keys2, ut_bf16, v_bf16):
    qq, h_bf16 = _norm_proj(x, norm_gain, w_q, emit_h=True)
    a_idx, b_idx, g = _peer_select(qq, keys1, keys2)
    return _peer_mix(a_idx, b_idx, g, h_bf16, ut_bf16, v_bf16, x)


def kernel(x_prompt, x_sample, state_hgrn, cache_k, cache_v, page_table, norm_mix, norm_ffn, hgrn_w_in, hgrn_lb,
           hgrn_o_norm, hgrn_w_out, moba_w_in, moba_q_norm, moba_k_norm, moba_w_out, rel_bias, peer_w_q,
           peer_keys1, peer_keys2, peer_u, peer_v):
    depth = norm_mix.shape[0]
    Bp, Tp, D = x_prompt.shape
    Bs, Ts, _ = x_sample.shape
    assert Bp == 1 and Ts == 1 and D == D_MODEL
    H = N_HEADS
    lbs = jnp.cumsum(jax.nn.softmax(hgrn_lb.astype(F32), axis=0), axis=0)
    xp = x_prompt.reshape(Tp, D)
    xs = x_sample.reshape(Bs, D)
    st_p, st_s, kp_rows, vp_rows, ks_rows, vs_rows = [], [], [], [], [], []
    for layer in range(depth):
        if layer % 2 == 0:
            a = layer // 2
            w_in = hgrn_w_in[a].astype(BF16)
            w_out = hgrn_w_out[a].astype(BF16)
            proj_p, = _norm_proj(xp, norm_mix[layer], w_in)
            proj_s, = _norm_proj(xs, norm_mix[layer], w_in)
            o_p, s_p = _hgrn_prompt(proj_p, lbs[a])
            o_s, s_s = _hgrn_step(proj_s, lbs[a], state_hgrn[a])
            xp = _out_proj(o_p, w_out, xp, gate_src=proj_p, gate_col=3, gain=hgrn_o_norm[a])
            xs = _out_proj(o_s, w_out, xs, gate_src=proj_s, gate_col=3, gain=hgrn_o_norm[a])
            st_p.append(s_p[None])
            st_s.append(s_s)
        else:
            b = layer // 2
            w_in = moba_w_in[b].astype(BF16)
            w_out = moba_w_out[b].astype(BF16)
            head_gain = jnp.concatenate([jnp.tile(moba_q_norm[b], H), jnp.tile(moba_k_norm[b], H), jnp.ones((D,), F32)])
            q_p, k_p, v_p, k_p16, vt_p16, kmean_p = _moba_qkv(xp, norm_mix[layer], w_in, moba_q_norm[b], moba_k_norm[b])
            qkv_s, = _norm_proj(xs, norm_mix[layer], w_in, head_gain=head_gain, n_head_norm_tiles=2)
            att_p = _moba_prompt(q_p, k_p16, vt_p16, kmean_p, rel_bias)
            q_s = qkv_s[:, :D].reshape(Bs, H, HEAD_DIM)
            k_s = qkv_s[:, D:2 * D].reshape(Bs, H, HEAD_DIM)
            v_s = qkv_s[:, 2 * D:].reshape(Bs, H, HEAD_DIM)
            att_s = _moba_decode(q_s, k_s, v_s, cache_k, cache_v, b, page_table, rel_bias)
            xp = _out_proj(att_p, w_out, xp)
            xs = _out_proj(att_s, w_out, xs)
            kp_rows.append(k_p.reshape(Bp, Tp, H, HEAD_DIM))
            vp_rows.append(v_p.reshape(Bp, Tp, H, HEAD_DIM))
            ks_rows.append(k_s.reshape(Bs, Ts, H, HEAD_DIM))
            vs_rows.append(v_s.reshape(Bs, Ts, H, HEAD_DIM))
        w_q = peer_w_q[layer].astype(BF16)
        u16 = peer_u[layer].astype(BF16).T
        v16 = peer_v[layer].astype(BF16)
        xp = _peer(xp, norm_ffn[layer], w_q, peer_keys1[layer], peer_keys2[layer], u16, v16)
        xs = _peer(xs, norm_ffn[layer], w_q, peer_keys1[layer], peer_keys2[layer], u16, v16)
    return (xp.reshape(Bp, Tp, D), xs.reshape(Bs, Ts, D), jnp.stack(st_p), jnp.stack(st_s),
            jnp.stack(kp_rows), jnp.stack(vp_rows), jnp.stack(ks_rows), jnp.stack(vs_rows))
```

```python
import functools
import math

import numpy as np
import jax
import jax.numpy as jnp
from jax import lax
from jax.experimental import pallas as pl
from jax.experimental.pallas import tpu as pltpu

F32 = jnp.float32
BF16 = jnp.bfloat16
I32 = jnp.int32

D_MODEL = 1024
HEAD_DIM = 128
N_HEADS = D_MODEL // HEAD_DIM
PAGE_SIZE = 128
HGRN_CHUNK = 64
MOBA_BLOCK = 256
MOBA_TOPK = 3
REL_BUCKETS = 32
REL_MAX_DIST = 128
PEER_HEADS = 8
PEER_N_KEYS = 128
PEER_HALF = 128
PEER_TOPK = 16
PEER_PAIRS = PEER_HEADS * PEER_TOPK
RMS_EPS = 1e-6

LANES = 128
SUBLANES = 8
V7X_VMEM_BYTES = 64 * 1024 * 1024
NEG_BIG = -1e30
LOG2_E = 1.4426950408889634


def _vmem_limit(estimate_bytes):
    return int(min(max(2 * estimate_bytes, 16 * 1024 * 1024), V7X_VMEM_BYTES - 8 * 1024 * 1024))


def _params(sem, estimate_bytes):
    return pltpu.CompilerParams(dimension_semantics=sem, vmem_limit_bytes=_vmem_limit(estimate_bytes))


def _split2(x):
    hi = x.astype(BF16)
    lo = (x - hi.astype(F32)).astype(BF16)
    return hi, lo


def _split3(x):
    hi = x.astype(BF16)
    r = x - hi.astype(F32)
    mid = r.astype(BF16)
    lo = (r - mid.astype(F32)).astype(BF16)
    return hi, mid, lo


_NT = (((1,), (1,)), ((), ()))
_TN = (((0,), (0,)), ((), ()))


def _dot_nt(a, b):
    return lax.dot_general(a, b, _NT, preferred_element_type=F32)


def _dot_tn(a, b):
    return lax.dot_general(a, b, _TN, preferred_element_type=F32)


def _dot_nt_x3(a, b):
    ah, al = _split2(a)
    bh, bl = _split2(b)
    return _dot_nt(ah, bh) + _dot_nt(ah, bl) + _dot_nt(al, bh)


def _sigmoid(x):
    return 1.0 / (1.0 + jnp.exp(-x))


def _head_rms(y, gain):
    outs = []
    for h in range(y.shape[1] // HEAD_DIM):
        yh = y[:, h * HEAD_DIM:(h + 1) * HEAD_DIM]
        ms = jnp.mean(yh * yh, axis=-1, keepdims=True)
        outs.append(yh * lax.rsqrt(ms + RMS_EPS))
    return jnp.concatenate(outs, axis=-1) * gain


def _topk_rows(s, k, ids=None):
    if ids is None:
        ids = lax.broadcasted_iota(I32, s.shape, 0).astype(F32)
    vals, idxs = [], []
    cur = s
    for _ in range(k):
        m = jnp.max(cur, axis=0, keepdims=True)
        idx = jnp.min(jnp.where(cur == m, ids, jnp.inf), axis=0, keepdims=True)
        vals.append(m)
        idxs.append(idx)
        cur = jnp.where(ids == idx, -jnp.inf, cur)
    return jnp.concatenate(vals, axis=0), jnp.concatenate(idxs, axis=0)


def _norm_cast(x_ref, g_ref):
    x = x_ref[...]
    ms = jnp.mean(x * x, axis=-1, keepdims=True)
    return (x * lax.rsqrt(ms + RMS_EPS) * g_ref[...]).astype(BF16)


def _norm_proj_kernel(x_ref, g_ref, w_ref, hg_ref, y_ref, *h_out, n_head_norm_tiles):
    h = _norm_cast(x_ref, g_ref)
    y = jnp.dot(h, w_ref[...], preferred_element_type=F32)
    if n_head_norm_tiles:
        y = jnp.where(pl.program_id(1) < n_head_norm_tiles, _head_rms(y, hg_ref[...]), y)
    y_ref[...] = y
    if h_out:
        h_out[0][...] = h


def _norm_proj(x, gain, w_bf16, *, head_gain=None, n_head_norm_tiles=0, emit_h=False, tn=1024):
    T, D = x.shape
    N = w_bf16.shape[1]
    tm = min(T, 1024)
    assert T % tm == 0 and N % tn == 0
    if head_gain is None:
        head_gain = jnp.ones((N,), F32)
    out_shape = [jax.ShapeDtypeStruct((T, N), F32)]
    out_specs = [pl.BlockSpec((tm, tn), lambda i, j: (i, j))]
    if emit_h:
        out_shape.append(jax.ShapeDtypeStruct((T, D), BF16))
        out_specs.append(pl.BlockSpec((tm, D), lambda i, j: (i, 0)))
    est = 2 * (tm * D * 4 + D * tn * 2 + tm * tn * 4 + tm * D * 2) + tm * tn * 8
    return pl.pallas_call(
        functools.partial(_norm_proj_kernel, n_head_norm_tiles=n_head_norm_tiles),
        grid=(T // tm, N // tn),
        in_specs=[
            pl.BlockSpec((tm, D), lambda i, j: (i, 0)),
            pl.BlockSpec((1, D), lambda i, j: (0, 0)),
            pl.BlockSpec((D, tn), lambda i, j: (0, j)),
            pl.BlockSpec((1, tn), lambda i, j: (0, j)),
        ],
        out_specs=out_specs,
        out_shape=out_shape,
        compiler_params=_params(("parallel", "arbitrary"), est),
        name="norm_proj",
    )(x, gain.reshape(1, D), w_bf16, head_gain.reshape(1, N))


VT_ONES_ROWS = 16


def _moba_qkv_kernel(x_ref, g_ref, w_ref, hg_ref, wvt_ref, q_ref, k_ref, v_ref, kb_ref, vt_ref, kmean_ref):
    j = pl.program_id(1)
    tm = x_ref.shape[0]
    h = _norm_cast(x_ref, g_ref)
    y = jnp.dot(h, w_ref[...], preferred_element_type=F32)

    @pl.when(j == 0)
    def _():
        q_ref[...] = _head_rms(y, hg_ref[...])

    @pl.when(j == 1)
    def _():
        k = _head_rms(y, hg_ref[...])
        k_ref[...] = k
        kb_ref[...] = k.astype(BF16)
        for b in range(tm // MOBA_BLOCK):
            kmean_ref[b] = jnp.mean(k[b * MOBA_BLOCK:(b + 1) * MOBA_BLOCK], axis=0, keepdims=True)

    @pl.when(j == 2)
    def _():
        v_ref[...] = y
        vt = _dot_nt(wvt_ref[...], h).astype(BF16)
        vt_ref[:, :HEAD_DIM, :] = vt.reshape(N_HEADS, HEAD_DIM, tm)
        vt_ref[:, HEAD_DIM:, :] = jnp.ones((N_HEADS, VT_ONES_ROWS, tm), BF16)


def _moba_qkv(x, gain, w_bf16, q_gain, k_gain):
    T, D = x.shape
    H = N_HEADS
    tm = min(T, 1024)
    assert T % tm == 0 and tm % MOBA_BLOCK == 0 and w_bf16.shape == (D, 3 * D)
    nb = T // MOBA_BLOCK
    head_gain = jnp.concatenate([jnp.tile(q_gain, H), jnp.tile(k_gain, H), jnp.ones((D,), F32)]).reshape(1, 3 * D)
    tok = pl.BlockSpec((tm, D), lambda i, j: (i, 0))
    rows = HEAD_DIM + VT_ONES_ROWS
    est = 2 * (tm * D * (4 + 3 * 4 + 2) + 2 * D * D * 2 + H * rows * tm * 2) + 2 * tm * D * 4
    q, k, v, kb, vt, kmean = pl.pallas_call(
        _moba_qkv_kernel,
        grid=(T // tm, 3),
        in_specs=[
            tok,
            pl.BlockSpec((1, D), lambda i, j: (0, 0)),
            pl.BlockSpec((D, D), lambda i, j: (0, j)),
            pl.BlockSpec((1, D), lambda i, j: (0, j)),
            pl.BlockSpec((D, D), lambda i, j: (0, 0)),
        ],
        out_specs=[tok, tok, tok, tok,
                   pl.BlockSpec((H, rows, tm), lambda i, j: (0, 0, i)),
                   pl.BlockSpec((tm // MOBA_BLOCK, 1, D), lambda i, j: (i, 0, 0))],
        out_shape=[jax.ShapeDtypeStruct((T, D), F32)] * 3 + [
            jax.ShapeDtypeStruct((T, D), BF16),
            jax.ShapeDtypeStruct((H, rows, T), BF16),
            jax.ShapeDtypeStruct((nb, 1, D), F32)],
        compiler_params=_params(("parallel", "arbitrary"), est),
        name="moba_qkv",
    )(x, gain.reshape(1, D), w_bf16, head_gain, w_bf16[:, 2 * D:].T)
    return q, k, v, kb, vt, kmean.reshape(nb, D)


def _out_proj_kernel(o_ref, gate_ref, gain_ref, w_ref, x_ref, y_ref, *, gated):
    o = o_ref[...]
    if gated:
        g = gate_ref[...]
        o = _head_rms(o, gain_ref[...]) * (g * _sigmoid(g))
    y_ref[...] = x_ref[...] + jnp.dot(o.astype(BF16), w_ref[...], preferred_element_type=F32)


def _out_proj(o, w_bf16, x, *, gate_src=None, gate_col=0, gain=None):
    T, D = o.shape
    tm = min(T, 512)
    assert T % tm == 0
    gated = gate_src is not None
    if not gated:
        gate_src, gain = o, jnp.ones((D,), F32)
    est = 2 * (4 * tm * D * 4 + D * D * 2)
    return pl.pallas_call(
        functools.partial(_out_proj_kernel, gated=gated),
        grid=(T // tm,),
        in_specs=[
            pl.BlockSpec((tm, D), lambda i: (i, 0)),
            pl.BlockSpec((tm, D), lambda i: (i, gate_col)),
            pl.BlockSpec((1, D), lambda i: (0, 0)),
            pl.BlockSpec((D, D), lambda i: (0, 0)),
            pl.BlockSpec((tm, D), lambda i: (i, 0)),
        ],
        out_specs=pl.BlockSpec((tm, D), lambda i: (i, 0)),
        out_shape=jax.ShapeDtypeStruct((T, D), F32),
        compiler_params=_params(("parallel",), est),
        name="out_proj",
    )(o, gate_src, gain.reshape(1, D), w_bf16, x)


HGRN_SAFE_DECAY = 80.0


def _hgrn_prompt_kernel(q_ref, fl_ref, v_ref, lb_ref, o_ref, s_out_ref, st_scr, b_scr, k_scr):
    c = pl.program_id(0)
    C = q_ref.shape[0]

    @pl.when(c == 0)
    def _():
        st_scr[...] = jnp.zeros_like(st_scr)

    lb = lb_ref[...]
    f = lb + (1.0 - lb) * _sigmoid(fl_ref[...])
    logf = jnp.log(f)
    q = q_ref[...]
    qq = q * _sigmoid(q)
    tri = (lax.broadcasted_iota(I32, (C, C), 0) >= lax.broadcasted_iota(I32, (C, C), 1)).astype(BF16)
    p0, p1, p2 = _split3(logf)
    b = (jnp.dot(tri, p0, preferred_element_type=F32) + jnp.dot(tri, p1, preferred_element_type=F32)
         + jnp.dot(tri, p2, preferred_element_type=F32))
    b_scr[...] = b
    k_scr[...] = 1.0 - f
    row = lax.broadcasted_iota(I32, (C, HEAD_DIM), 0)
    col = lax.broadcasted_iota(I32, (C, C), 1)
    causal = lax.broadcasted_iota(I32, (C, C), 0) >= col

    def head_step(h, hs, a_mat):
        bh = b_scr[:, hs]
        vh = v_ref[:, hs].astype(BF16)
        bl = bh[C - 1:C, :]
        qh = q_ref[:, hs]
        qh = qh * _sigmoid(qh)
        st = st_scr[h]
        inter = _dot_nt((qh * jnp.exp(bh)).astype(BF16), st.astype(BF16))
        intra = jnp.dot(a_mat.astype(BF16), vh, preferred_element_type=F32)
        o_ref[:, hs] = inter + intra
        k_dec = (k_scr[:, hs] * jnp.exp(bl - bh)).astype(BF16)
        st_scr[h] = st * jnp.exp(bl) + _dot_tn(vh, k_dec)

    safe = jnp.min(b[C - 1:C, :]) >= -HGRN_SAFE_DECAY

    @pl.when(safe)
    def _():
        for h in range(N_HEADS):
            hs = slice(h * HEAD_DIM, (h + 1) * HEAD_DIM)
            bh = b[:, hs]
            half = 0.5 * bh[C - 1:C, :]
            q_dec = (qq[:, hs] * jnp.exp(bh - half)).astype(BF16)
            k_inc = (k_scr[:, hs] * jnp.exp(half - bh)).astype(BF16)
            head_step(h, hs, jnp.where(causal, _dot_nt(q_dec, k_inc), 0.0))

    @pl.when(jnp.logical_not(safe))
    def _():
        for h in range(N_HEADS):
            hs = slice(h * HEAD_DIM, (h + 1) * HEAD_DIM)
            bh = b[:, hs]
            qh = qq[:, hs]

            def pair_cols(s8, a_mat):
                start = pl.multiple_of(s8 * SUBLANES, SUBLANES)
                b8 = b_scr[pl.ds(start, SUBLANES), hs]
                k8 = k_scr[pl.ds(start, SUBLANES), hs]
                for r in range(SUBLANES):
                    s = start + r
                    d = jnp.exp(jnp.where(row >= s, bh - b8[r:r + 1], -jnp.inf))
                    a_col = jnp.sum(qh * k8[r:r + 1] * d, axis=-1, keepdims=True)
                    a_mat = jnp.where(col == s, a_col, a_mat)
                return a_mat

            head_step(h, hs, lax.fori_loop(0, C // SUBLANES, pair_cols, jnp.zeros((C, C), F32)))

    @pl.when(c == pl.num_programs(0) - 1)
    def _():
        for h in range(N_HEADS):
            s_out_ref[h] = st_scr[h].T


def _hgrn_prompt(proj, lb):
    T = proj.shape[0]
    C = min(HGRN_CHUNK, T)
    assert T % C == 0 and C % SUBLANES == 0
    D = D_MODEL
    est = 2 * 4 * C * D * 4 + 3 * N_HEADS * HEAD_DIM * HEAD_DIM * 4 + 2 * C * D * 4
    return pl.pallas_call(
        _hgrn_prompt_kernel,
        grid=(T // C,),
        in_specs=[
            pl.BlockSpec((C, D), lambda c: (c, 0)),
            pl.BlockSpec((C, D), lambda c: (c, 1)),
            pl.BlockSpec((C, D), lambda c: (c, 2)),
            pl.BlockSpec((1, D), lambda c: (0, 0)),
        ],
        out_specs=[
            pl.BlockSpec((C, D), lambda c: (c, 0)),
            pl.BlockSpec((N_HEADS, HEAD_DIM, HEAD_DIM), lambda c: (0, 0, 0)),
        ],
        out_shape=[
            jax.ShapeDtypeStruct((T, D), F32),
            jax.ShapeDtypeStruct((N_HEADS, HEAD_DIM, HEAD_DIM), F32),
        ],
        scratch_shapes=[
            pltpu.VMEM((N_HEADS, HEAD_DIM, HEAD_DIM), F32),
            pltpu.VMEM((C, D), F32),
            pltpu.VMEM((C, D), F32),
        ],
        compiler_params=_params(("arbitrary",), est),
        name="hgrn_prompt",
    )(proj, proj, proj, lb.reshape(1, D))


def _hgrn_step_kernel(q_ref, fl_ref, lb_ref, v_ref, s_ref, o_ref, s_out_ref):
    lb = lb_ref[...]
    f = lb + (1.0 - lb) * _sigmoid(fl_ref[...])
    q = q_ref[...]
    qq = q * _sigmoid(q)
    s_new = s_ref[...] * f + (1.0 - f) * v_ref[...]
    s_out_ref[...] = s_new
    o_ref[...] = jnp.sum(s_new * qq, axis=1, keepdims=True)


def _hgrn_step(proj, lb, state):
    B = proj.shape[0]
    H, DK = N_HEADS, HEAD_DIM
    q = proj[:, :D_MODEL].reshape(B, H, DK, 1)
    fl = proj[:, D_MODEL:2 * D_MODEL].reshape(B, H, DK, 1)
    v = proj[:, 2 * D_MODEL:3 * D_MODEL].reshape(B, H, 1, DK)
    col_spec = pl.BlockSpec((None, H, DK, 1), lambda b: (b, 0, 0, 0))
    row_spec = pl.BlockSpec((None, H, 1, DK), lambda b: (b, 0, 0, 0))
    est = 2 * (3 * H * DK * LANES * 4 + 2 * H * DK * DK * 4)
    o, s_new = pl.pallas_call(
        _hgrn_step_kernel,
        grid=(B,),
        in_specs=[
            col_spec, col_spec,
            pl.BlockSpec((H, DK, 1), lambda b: (0, 0, 0)),
            row_spec,
            pl.BlockSpec((None, H, DK, DK), lambda b: (b, 0, 0, 0)),
        ],
        out_specs=[row_spec, pl.BlockSpec((None, H, DK, DK), lambda b: (b, 0, 0, 0))],
        out_shape=[jax.ShapeDtypeStruct((B, H, 1, DK), F32), jax.ShapeDtypeStruct(state.shape, F32)],
        compiler_params=_params(("parallel",), est),
        name="hgrn_step",
    )(q, fl, lb.reshape(H, DK, 1), v, state)
    return o.reshape(B, D_MODEL), s_new


def _rel_bucket_table():
    d = np.arange(REL_MAX_DIST + 1)
    max_exact = REL_BUCKETS // 2
    large = max_exact + (np.log(np.maximum(d, max_exact).astype(np.float32) / max_exact)
                         / math.log(REL_MAX_DIST / max_exact) * (REL_BUCKETS - max_exact)).astype(np.int32)
    large = np.minimum(large, REL_BUCKETS - 1)
    tab = np.where(d < max_exact, d, large).astype(np.int32)
    assert tab[REL_MAX_DIST] == REL_BUCKETS - 1
    return tab


MOBA_FAR_GROUP = 4


def _moba_prompt_kernel(q_ref, kmean_ref, k_ref, vt_ref, bias_ref, far_ref, o_ref, sel_scr, m_scr, acc_scr,
                        s_scr, *, group):
    c0 = 2 * pl.program_id(1)
    tq = q_ref.shape[0]
    nb = kmean_ref.shape[0]
    q = q_ref[...]
    qs = (q * (HEAD_DIM ** -0.5 * LOG2_E)).astype(BF16)

    gate = _dot_nt_x3(kmean_ref[...], q)
    blk = lax.broadcasted_iota(I32, (nb, tq), 0)
    past = blk < c0 + (lax.broadcasted_iota(I32, (nb, tq), 1) >= MOBA_BLOCK).astype(I32)
    cur = jnp.where(past, gate, -jnp.inf)
    sel = jnp.zeros((nb, tq), F32)
    for _ in range(MOBA_TOPK):
        m = jnp.max(cur, axis=0, keepdims=True)
        idx = jnp.min(jnp.where((cur == m) & past, blk, nb), axis=0, keepdims=True)
        hit = (blk == idx) & (cur > -jnp.inf)
        sel = jnp.where(hit, 1.0, sel)
        cur = jnp.where(blk == idx, -jnp.inf, cur)
    pad = sel_scr.shape[0] - nb
    sel_scr[...] = jnp.concatenate([sel, jnp.zeros((pad, tq), F32)], axis=0) if pad else sel
    sub = lax.broadcasted_iota(I32, (SUBLANES, tq), 0)

    def allowed(n):
        start = pl.multiple_of((n // SUBLANES) * SUBLANES, SUBLANES)
        rows = sel_scr[pl.ds(start, SUBLANES), :]
        return jnp.max(jnp.where(sub == n - start, rows, 0.0), axis=0, keepdims=True) > 0.0

    m_scr[...] = jnp.full_like(m_scr, NEG_BIG)
    acc_scr[...] = jnp.zeros_like(acc_scr)

    far = far_ref[0:1, 0:1]

    n_groups_total = nb // group

    def group_start(ng):
        return pl.multiple_of(jnp.minimum(ng, n_groups_total - 1) * (group * MOBA_BLOCK), group * MOBA_BLOCK)

    def far_scores(ng, slot):
        s_scr[slot] = _dot_nt(k_ref[pl.ds(group_start(ng), group * MOBA_BLOCK), :], qs)

    def far_update(ng, slot):
        n0 = ng * group
        start = group_start(ng)
        parts = [s_scr[slot, g * MOBA_BLOCK:(g + 1) * MOBA_BLOCK, :] for g in range(group)]
        allow = [allowed(jnp.minimum(n0 + g, nb - 1)) & (n0 + g < c0 - 1) for g in range(group)]
        mx = jnp.where(allow[0], jnp.max(parts[0], axis=0, keepdims=True), NEG_BIG)
        for g in range(1, group):
            mx = jnp.maximum(mx, jnp.where(allow[g], jnp.max(parts[g], axis=0, keepdims=True), NEG_BIG))
        m_old = m_scr[...]
        m_new = jnp.maximum(m_old, mx + far)
        alpha = jnp.exp2(m_old - m_new)
        shift = m_new - far
        acc = alpha * acc_scr[...]
        for g in range(group):
            p = jnp.exp2(parts[g] - shift)
            start_g = pl.multiple_of(start + g * MOBA_BLOCK, MOBA_BLOCK)
            pv = jnp.dot(vt_ref[:, pl.ds(start_g, MOBA_BLOCK)], p.astype(BF16), preferred_element_type=F32)
            acc = acc + jnp.where(allow[g], pv, 0.0)
        acc_scr[...] = acc
        m_scr[...] = m_new

    n_groups = (c0 + group - 2) // group
    far_scores(0, 0)

    def far_pair(p, carry):
        far_scores(2 * p + 1, 1)
        far_update(2 * p, 0)
        far_scores(2 * p + 2, 0)
        far_update(2 * p + 1, 1)
        return carry

    lax.fori_loop(0, (n_groups + 1) // 2, far_pair, 0)

    tile = (MOBA_BLOCK, tq)
    row = lax.broadcasted_iota(I32, tile, 0)
    lane = lax.broadcasted_iota(I32, tile, 1)
    first = lane < MOBA_BLOCK
    causal_first = first & (row <= lane)
    causal_second = jnp.logical_not(first) & (row <= lane - MOBA_BLOCK)
    bias_own, bias_prev = bias_ref[0], bias_ref[1]

    def near_pass(start_block, pieces):
        start = pl.multiple_of(start_block * MOBA_BLOCK, MOBA_BLOCK)
        s = _dot_nt(k_ref[pl.ds(start, len(pieces) * MOBA_BLOCK), :], qs)
        parts = [s[g * MOBA_BLOCK:(g + 1) * MOBA_BLOCK] + bias for g, (bias, _) in enumerate(pieces)]
        mx = None
        for part, (_, allow) in zip(parts, pieces):
            col_max = jnp.max(jnp.where(allow, part, NEG_BIG), axis=0, keepdims=True)
            mx = col_max if mx is None else jnp.maximum(mx, col_max)
        m_old = m_scr[...]
        m_new = jnp.maximum(m_old, mx)
        acc = jnp.exp2(m_old - m_new) * acc_scr[...]
        for g, (part, (_, allow)) in enumerate(zip(parts, pieces)):
            p = jnp.where(allow, jnp.exp2(part - m_new), 0.0).astype(BF16)
            start_g = pl.multiple_of(start + g * MOBA_BLOCK, MOBA_BLOCK)
            acc = acc + jnp.dot(vt_ref[:, pl.ds(start_g, MOBA_BLOCK)], p, preferred_element_type=F32)
        acc_scr[...] = acc
        m_scr[...] = m_new

    def own_pieces():
        return [(jnp.concatenate([bias_own, bias_prev], axis=1), causal_first | (jnp.logical_not(first) & allowed(c0))),
                (jnp.concatenate([bias_own, bias_own], axis=1), causal_second)]

    @pl.when(c0 == 0)
    def _():
        near_pass(0, own_pieces())

    @pl.when(c0 >= 2)
    def _():
        far_tile = jnp.broadcast_to(far, (MOBA_BLOCK, MOBA_BLOCK))
        before = (jnp.concatenate([bias_prev, far_tile], axis=1), jnp.broadcast_to(allowed(c0 - 1), tile))
        near_pass(c0 - 1, [before] + own_pieces())

    o_ref[...] = (acc_scr[:HEAD_DIM, :] / acc_scr[HEAD_DIM:HEAD_DIM + 1, :]).T


def _moba_prompt(q, k_bf16, vt_ones, kmean, rel_bias):
    T = q.shape[0]
    assert T % MOBA_BLOCK == 0
    nb = T // MOBA_BLOCK
    H = N_HEADS
    tab =rel_bias[jnp.asarray(_rel_bucket_table())] * LOG2_E
    n = MOBA_BLOCK
    e = np.arange(2 * n)
    e_signed = np.where(e < n, e, e - 2 * n)
    w_own = tab[np.clip(e_signed, 0, REL_MAX_DIST)]
    w_prev = tab[np.clip(n + e_signed, 0, REL_MAX_DIST)]

    def toeplitz(w):
        return jnp.tile(w, n)[:n * (2 * n - 1)].reshape(n, 2 * n - 1)[:, :n]

    near = jnp.stack([jax.vmap(toeplitz, in_axes=1)(w_own), jax.vmap(toeplitz, in_axes=1)(w_prev)], axis=1)
    far = jnp.broadcast_to(tab[REL_MAX_DIST][:, None, None], (H, 8, LANES))
    group = math.gcd(nb, MOBA_FAR_GROUP)
    assert nb % 2 == 0
    tq = 2 * MOBA_BLOCK
    est = 2 * (2 * T * HEAD_DIM * 2 + 2 * MOBA_BLOCK * MOBA_BLOCK * 4) + (24 + 4 * group) * MOBA_BLOCK * tq * 4
    return pl.pallas_call(
        functools.partial(_moba_prompt_kernel, group=group),
        grid=(H, nb // 2),
        in_specs=[
            pl.BlockSpec((tq, HEAD_DIM), lambda h, i: (i, h)),
            pl.BlockSpec((nb, HEAD_DIM), lambda h, i: (0, h)),
            pl.BlockSpec((T, HEAD_DIM), lambda h, i: (0, h)),
            pl.BlockSpec((None, HEAD_DIM + VT_ONES_ROWS, T), lambda h, i: (h, 0, 0)),
            pl.BlockSpec((None, 2, MOBA_BLOCK, MOBA_BLOCK), lambda h, i: (h, 0, 0, 0)),
            pl.BlockSpec((None, 8, LANES), lambda h, i: (h, 0, 0)),
        ],
        out_specs=pl.BlockSpec((tq, HEAD_DIM), lambda h, i: (i, h)),
        out_shape=jax.ShapeDtypeStruct((T, D_MODEL), F32),
        scratch_shapes=[
            pltpu.VMEM((-(-nb // SUBLANES) * SUBLANES, tq), F32),
            pltpu.VMEM((1, tq), F32),
            pltpu.VMEM((HEAD_DIM + VT_ONES_ROWS, tq), F32),
            pltpu.VMEM((2, group * MOBA_BLOCK, tq), F32),
        ],
        compiler_params=_params(("parallel", "arbitrary"), est),
        name="moba_prompt",
    )(q, kmean, k_bf16, vt_ones, near, far)


PAGES_PER_BLOCK = MOBA_BLOCK // PAGE_SIZE
MEAN_BLOCKS_PER_STEP = 4


def _paged_block_mean_kernel(pt_ref, *refs):
    page_refs, o_ref = refs[:-1], refs[-1]
    for n in range(len(page_refs) // PAGES_PER_BLOCK):
        tot = jnp.sum(page_refs[n * PAGES_PER_BLOCK][...], axis=0)
        for p in range(1, PAGES_PER_BLOCK):
            tot = tot + jnp.sum(page_refs[n * PAGES_PER_BLOCK + p][...], axis=0)
        o_ref[n] = tot * (1.0 / MOBA_BLOCK)


def _paged_block_mean(cache_k, layer, page_table):
    B, n_pages = page_table.shape
    assert n_pages % PAGES_PER_BLOCK == 0
    nb = n_pages // PAGES_PER_BLOCK
    bps = math.gcd(nb, MEAN_BLOCKS_PER_STEP)
    pps = bps * PAGES_PER_BLOCK
    page_shape = (None, None, PAGE_SIZE, N_HEADS, HEAD_DIM)

    def page_spec(p):
        return pl.BlockSpec(page_shape, lambda b, n, pt: (layer, pt[b * n_pages + n * pps + p], 0, 0, 0))

    grid_spec = pltpu.PrefetchScalarGridSpec(
        num_scalar_prefetch=1,
        grid=(B, nb // bps),
        in_specs=[page_spec(p) for p in range(pps)],
        out_specs=pl.BlockSpec((None, bps, N_HEADS, HEAD_DIM), lambda b, n, pt: (b, n, 0, 0)),
    )
    return pl.pallas_call(
        _paged_block_mean_kernel,
        grid_spec=grid_spec,
        out_shape=jax.ShapeDtypeStruct((B, nb, N_HEADS, HEAD_DIM), F32),
        compiler_params=_params(("parallel", "arbitrary"), 2 * pps * PAGE_SIZE * D_MODEL * 4),
        name="moba_paged_block_mean",
    )(page_table.reshape(-1), *([cache_k] * pps))


def _decode_gate_kernel(q_ref, kmean_ref, o_ref):
    nb = kmean_ref.shape[0]
    gate = jnp.sum(kmean_ref[...] * q_ref[...][None], axis=-1, keepdims=True)
    blk = lax.broadcasted_iota(I32, gate.shape, 0)
    cur = gate
    for r in range(MOBA_TOPK):
        m = jnp.max(cur, axis=0, keepdims=True)
        idx = jnp.min(jnp.where(cur == m, blk, nb), axis=0, keepdims=True)
        o_ref[r] = jnp.broadcast_to(idx[0], (N_HEADS, LANES))
        cur = jnp.where(blk == idx, -jnp.inf, cur)


def _decode_gate(q, kmean):
    B, nb = kmean.shape[:2]
    out = pl.pallas_call(
        _decode_gate_kernel,
        grid=(B,),
        in_specs=[
            pl.BlockSpec((None, N_HEADS, HEAD_DIM), lambda b: (b, 0, 0)),
            pl.BlockSpec((None, nb, N_HEADS, HEAD_DIM), lambda b: (b, 0, 0, 0)),
        ],
        out_specs=pl.BlockSpec((None, MOBA_TOPK, N_HEADS, LANES), lambda b: (b, 0, 0, 0)),
        out_shape=jax.ShapeDtypeStruct((B, MOBA_TOPK, N_HEADS, LANES), I32),
        compiler_params=_params(("parallel",), 4 * nb * N_HEADS * HEAD_DIM * 4),
        name="moba_decode_gate",
    )(q, kmean)
    return out[..., 0].transpose(0, 2, 1)


def _decode_attend_kernel(pages_ref, blocks_ref, q_ref, kn_ref, vn_ref, tab_ref, *refs, past_len):
    n_src = MOBA_TOPK * 2
    k_refs, v_refs, o_ref = refs[:n_src], refs[n_src:2 * n_src], refs[2 * n_src]
    b, h = pl.program_id(0), pl.program_id(1)
    head_rows = pl.ds(h, PAGE_SIZE, stride=N_HEADS)
    scale = HEAD_DIM ** -0.5
    q = q_ref[...]
    tab = tab_ref[...]
    row = lax.broadcasted_iota(I32, (PAGE_SIZE, 1), 0)
    lane = lax.broadcasted_iota(I32, (PAGE_SIZE, 2 * LANES), 1)
    logits = []
    for j in range(n_src):
        blk = blocks_ref[(b * N_HEADS + h) * MOBA_TOPK + j // 2]
        rel = past_len - (blk * MOBA_BLOCK + (j % 2) * PAGE_SIZE + row)
        bias = jnp.sum(jnp.where(lane == jnp.minimum(rel, REL_MAX_DIST), tab, 0.0), axis=-1, keepdims=True)
        logits.append(jnp.sum(k_refs[j][head_rows, :] * q, axis=-1, keepdims=True) * scale + bias)
    own = jnp.sum(kn_ref[...] * q, axis=-1, keepdims=True) * scale + tab[:, 0:1]
    m = own
    for s in logits:
        m = jnp.maximum(m, jnp.max(s, axis=0, keepdims=True))
    p_own = jnp.exp(own - m)
    l = p_own
    acc = p_own * vn_ref[...]
    for j in range(n_src):
        p = jnp.exp(logits[j] - m)
        l = l + jnp.sum(p, axis=0, keepdims=True)
        acc = acc + jnp.sum(p * v_refs[j][head_rows, :], axis=0, keepdims=True)
    o_ref[...] = acc / l


def _moba_decode(q, k_new, v_new, cache_k, cache_v, layer, page_table, rel_bias):
    B, n_pages = page_table.shape
    past_len = n_pages * PAGE_SIZE
    assert past_len % MOBA_BLOCK == 0 and past_len // MOBA_BLOCK >= MOBA_TOPK
    H = N_HEADS
    kmean = _paged_block_mean(cache_k, layer, page_table)
    blocks = _decode_gate(q, kmean)
    pages = jnp.take_along_axis(
        page_table[:, None, :], (blocks[..., None] * 2 + jnp.arange(2, dtype=I32)).reshape(B, H, -1), axis=2)
    n_src = MOBA_TOPK * 2
    tab = rel_bias[jnp.asarray(_rel_bucket_table())].T
    tab = jnp.pad(tab, ((0, 0), (0, 2 * LANES - tab.shape[1]))).reshape(H, 1, 2 * LANES)
    n_layers, n_pool = cache_k.shape[:2]
    ck = cache_k.reshape(n_layers, n_pool, PAGE_SIZE * H, HEAD_DIM)
    cv = cache_v.reshape(n_layers, n_pool, PAGE_SIZE * H, HEAD_DIM)

    def page_spec(j):
        return pl.BlockSpec((None, None, PAGE_SIZE * H, HEAD_DIM),
                            lambda b, h, pg, bk: (layer, pg[(b * H + h) * n_src + j], 0, 0))

    vec_spec = pl.BlockSpec((None, None, 1, HEAD_DIM), lambda b, h, pg, bk: (b, h, 0, 0))
    grid_spec = pltpu.PrefetchScalarGridSpec(
        num_scalar_prefetch=2,
        grid=(B, H),
        in_specs=[vec_spec, vec_spec, vec_spec,
                  pl.BlockSpec((None, 1, 2 * LANES), lambda b, h, pg, bk: (h, 0, 0))]
                 + [page_spec(j) for j in range(n_src)] * 2,
        out_specs=vec_spec,
    )
    r4 = lambda a: a.reshape(B, H, 1, HEAD_DIM)
    out = pl.pallas_call(
        functools.partial(_decode_attend_kernel, past_len=past_len),
        grid_spec=grid_spec,
        out_shape=jax.ShapeDtypeStruct((B, H, 1, HEAD_DIM), F32),
        compiler_params=_params(("parallel", "arbitrary"), 4 * n_src * PAGE_SIZE * HEAD_DIM * 4),
        name="moba_decode_attend",
    )(pages.reshape(-1), blocks.reshape(-1), r4(q), r4(k_new), r4(v_new), tab,
      *([ck] * n_src), *([cv] * n_src))
    return out.reshape(B, D_MODEL)


def _peer_select_kernel(qq_ref, k1_ref, k2_ref, a_ref, b_ref, g_ref):
    tq = qq_ref.shape[0]
    k1 = k1_ref[...]
    k2 = k2_ref[...]
    a_rows, b_rows, g_rows = [], [], []
    for h in range(PEER_HEADS):
        q1 = qq_ref[:, (2 * h) * PEER_HALF:(2 * h + 1) * PEER_HALF]
        q2 = qq_ref[:, (2 * h + 1) * PEER_HALF:(2 * h + 2) * PEER_HALF]
        v1, i1 = _topk_rows(_dot_nt_x3(k1, q1), PEER_TOPK)
        v2, i2 = _topk_rows(_dot_nt_x3(k2, q2), PEER_TOPK)
        cand, pos_ids = [], []
        for r in range(PEER_TOPK):
            cnt = PEER_TOPK // (r + 1)
            cand.append(v1[r:r + 1] + v2[:cnt])
            pos_ids.append((lax.broadcasted_iota(I32, (cnt, tq), 0) + r * PEER_TOPK).astype(F32))
        sc, pos = _topk_rows(jnp.concatenate(cand, axis=0), PEER_TOPK, jnp.concatenate(pos_ids, axis=0))
        pos = pos.astype(I32)
        r_k = pos >> 4
        c_k = pos & (PEER_TOPK - 1)
        a_k = jnp.zeros_like(sc)
        b_k = jnp.zeros_like(sc)
        for r in range(PEER_TOPK):
            a_k = jnp.where(r_k == r, i1[r:r + 1], a_k)
            b_k = jnp.where(c_k == r, i2[r:r + 1], b_k)
        e = jnp.exp(sc - sc[0:1])
        g_rows.append(e / jnp.sum(e, axis=0, keepdims=True))
        a_rows.append(a_k)
        b_rows.append(b_k)
    a_ref[...] = jnp.concatenate(a_rows, axis=0).T
    b_ref[...] = jnp.concatenate(b_rows, axis=0).T
    g_ref[...] = jnp.concatenate(g_rows, axis=0).T


def _peer_select(qq, keys1, keys2):
    T = qq.shape[0]
    tq = min(T, 256)
    assert T % tq == 0
    spec = pl.BlockSpec((tq, PEER_PAIRS), lambda i: (i, 0))
    key_spec = pl.BlockSpec((PEER_N_KEYS, PEER_HALF), lambda i: (0, 0))
    shape = jax.ShapeDtypeStruct((T, PEER_PAIRS), F32)
    return pl.pallas_call(
        _peer_select_kernel,
        grid=(T // tq,),
        in_specs=[pl.BlockSpec((tq, 2 * PEER_HEADS * PEER_HALF), lambda i: (i, 0)), key_spec, key_spec],
        out_specs=[spec, spec, spec],
        out_shape=[shape, shape, shape],
        compiler_params=_params(("parallel",), 4 * tq * 2048 * 4),
        name="peer_select",
    )(qq, keys1, keys2)


def _gelu_exact(x):
    return 0.5 * x * (1.0 + lax.erf(x * (2.0 ** -0.5)))


PEER_TOKEN_TILE = 512
PEER_BUILD_ROWS = 16
PEER_KEY1_PER_TILE = 16


def _peer_mix_kernel(a_ref, b_ref, g_ref, h_ref, ut_ref, v_ref, x_ref, y_ref, gmat_scr, acc_scr, stage_scr, *, groups):
    j = pl.program_id(1)
    tq = h_ref.shape[0]
    rows = PEER_BUILD_ROWS

    @pl.when(j == 0)
    def _():
        acc_scr[...] = jnp.zeros_like(acc_scr)
        key_b = lax.broadcasted_iota(I32, (PEER_N_KEYS, PEER_PAIRS), 0).astype(F32).astype(BF16)
        one_b = jnp.ones((PEER_N_KEYS, PEER_PAIRS), BF16)
        zero_b = jnp.zeros((PEER_N_KEYS, PEER_PAIRS), BF16)

        n_row_groups = tq // rows

        def token_mats(tg, slot):
            t0 = pl.multiple_of(jnp.minimum(tg, n_row_groups - 1) * rows, rows)
            a8 = a_ref[pl.ds(t0, rows), :]
            b8 = b_ref[pl.ds(t0, rows), :]
            g8 = g_ref[pl.ds(t0, rows), :].astype(BF16)
            for r in range(rows):
                a_hot = jnp.where(key_b == a8[r:r + 1].astype(BF16), one_b, zero_b)
                b_hot = jnp.where(key_b == b8[r:r + 1].astype(BF16), g8[r:r + 1], zero_b)
                stage_scr[slot, r] = _dot_nt(a_hot, b_hot).astype(BF16)

        def file_mats(tg, slot):
            t0 = pl.multiple_of(tg * rows, rows)
            gmat_scr[:, pl.ds(t0, rows), :] = jnp.swapaxes(stage_scr[slot], 0, 1)

        token_mats(0, 0)

        def build(p, carry):
            token_mats(2 * p + 1, 1)
            file_mats(2 * p, 0)
            token_mats(2 * p + 2, 0)
            file_mats(2 * p + 1, 1)
            return carry

        lax.fori_loop(0, n_row_groups // 2, build, 0)

    act = _gelu_exact(jnp.dot(h_ref[...], ut_ref[...], preferred_element_type=F32))
    w = jnp.concatenate([gmat_scr[j * groups + a] for a in range(groups)], axis=-1)
    z = (act * w.astype(F32)).astype(BF16)
    acc_scr[...] += jnp.dot(z, v_ref[...], preferred_element_type=F32)

    @pl.when(j == pl.num_programs(1) - 1)
    def _():
        y_ref[...] = x_ref[...] + acc_scr[...]


def _peer_mix(a_idx, b_idx, g, h_bf16, ut_bf16, v_bf16, x):
    T, D = x.shape
    E = v_bf16.shape[0]
    tq = min(T, PEER_TOKEN_TILE)
    groups = PEER_KEY1_PER_TILE
    te = groups * PEER_N_KEYS
    assert T % tq == 0 and tq % (2 * PEER_BUILD_ROWS) == 0 and E % te == 0 and E == PEER_N_KEYS * PEER_N_KEYS
    pair_spec = pl.BlockSpec((tq, PEER_PAIRS), lambda i, j: (i, 0))
    tok_spec = pl.BlockSpec((tq, D), lambda i, j: (i, 0))
    est = tq * PEER_N_KEYS * PEER_N_KEYS * 2 + 2 * (2 * te * D * 2 + 3 * tq * D * 4) + 4 * tq * te * 4
    return pl.pallas_call(
        functools.partial(_peer_mix_kernel, groups=groups),
        grid=(T // tq, E // te),
        in_specs=[
            pair_spec, pair_spec, pair_spec, tok_spec,
            pl.BlockSpec((D, te), lambda i, j: (0, j)),
            pl.BlockSpec((te, D), lambda i, j: (j, 0)),
            tok_spec,
        ],
        out_specs=tok_spec,
        out_shape=jax.ShapeDtypeStruct((T, D), F32),
        scratch_shapes=[
            pltpu.VMEM((PEER_N_KEYS, tq, PEER_N_KEYS), BF16),
            pltpu.VMEM((tq, D), F32),
            pltpu.VMEM((2, PEER_BUILD_ROWS, PEER_N_KEYS, PEER_N_KEYS), BF16),
        ],
        compiler_params=_params(("parallel", "arbitrary"), est),
        name="peer_mix",
    )(a_idx, b_idx, g, h_bf16, ut_bf16, v_bf16, x)


def _peer(x, norm_gain, w_q, keys1, keys2, ut_bf16, v_bf16):
    qq, h_bf16 = _norm_proj(x, norm_gain, w_q, emit_h=True)
    a_idx, b_idx, g = _peer_select(qq, keys1, keys2)
    return _peer_mix(a_idx, b_idx, g, h_bf16, ut_bf16, v_bf16, x)


def kernel(x_prompt, x_sample, state_hgrn, cache_k, cache_v, page_table, norm_mix, norm_ffn, hgrn_w_in, hgrn_lb,
           hgrn_o_norm, hgrn_w_out, moba_w_in, moba_q_norm, moba_k_norm, moba_w_out, rel_bias, peer_w_q,
           peer_keys1, peer_keys2, peer_u, peer_v):
    depth = norm_mix.shape[0]
    Bp, Tp, D = x_prompt.shape
    Bs, Ts, _ = x_sample.shape
    assert Bp == 1 and Ts == 1 and D == D_MODEL
    H = N_HEADS
    lbs = jnp.cumsum(jax.nn.softmax(hgrn_lb.astype(F32), axis=0), axis=0)
    xp = x_prompt.reshape(Tp, D)
    xs = x_sample.reshape(Bs, D)
    st_p, st_s, kp_rows, vp_rows, ks_rows, vs_rows = [], [], [], [], [], []
    for layer in range(depth):
        if layer % 2 == 0:
            a = layer // 2
            w_in = hgrn_w_in[a].astype(BF16)
            w_out = hgrn_w_out[a].astype(BF16)
            proj_p, = _norm_proj(xp, norm_mix[layer], w_in)
            proj_s, = _norm_proj(xs, norm_mix[layer], w_in)
            o_p, s_p = _hgrn_prompt(proj_p, lbs[a])
            o_s, s_s = _hgrn_step(proj_s, lbs[a], state_hgrn[a])
            xp = _out_proj(o_p, w_out, xp, gate_src=proj_p, gate_col=3, gain=hgrn_o_norm[a])
            xs = _out_proj(o_s, w_out, xs, gate_src=proj_s, gate_col=3, gain=hgrn_o_norm[a])
            st_p.append(s_p[None])
            st_s.append(s_s)
        else:
            b = layer // 2
            w_in = moba_w_in[b].astype(BF16)
            w_out = moba_w_out[b].astype(BF16)
            head_gain = jnp.concatenate([jnp.tile(moba_q_norm[b], H), jnp.tile(moba_k_norm[b], H), jnp.ones((D,), F32)])
            q_p, k_p, v_p, k_p16, vt_p16, kmean_p = _moba_qkv(xp, norm_mix[layer], w_in, moba_q_norm[b], moba_k_norm[b])
            qkv_s, = _norm_proj(xs, norm_mix[layer], w_in, head_gain=head_gain, n_head_norm_tiles=2)
            att_p = _moba_prompt(q_p, k_p16, vt_p16, kmean_p, rel_bias)
            q_s = qkv_s[:, :D].reshape(Bs, H, HEAD_DIM)
            k_s = qkv_s[:, D:2 * D].reshape(Bs, H, HEAD_DIM)
            v_s = qkv_s[:, 2 * D:].reshape(Bs, H, HEAD_DIM)
            att_s = _moba_decode(q_s, k_s, v_s, cache_k, cache_v, b, page_table, rel_bias)
            xp = _out_proj(att_p, w_out, xp)
            xs = _out_proj(att_s, w_out, xs)
            kp_rows.append(k_p.reshape(Bp, Tp, H, HEAD_DIM))
            vp_rows.append(v_p.reshape(Bp, Tp, H, HEAD_DIM))
            ks_rows.append(k_s.reshape(Bs, Ts, H, HEAD_DIM))
            vs_rows.append(v_s.reshape(Bs, Ts, H, HEAD_DIM))
        w_q = peer_w_q[layer].astype(BF16)
        u16 = peer_u[layer].astype(BF16).T
        v16 = peer_v[layer].astype(BF16)
        xp = _peer(xp, norm_ffn[layer], w_q, peer_keys1[layer], peer_keys2[layer], u16, v16)
        xs = _peer(xs, norm_ffn[layer], w_q, peer_keys1[layer], peer_keys2[layer], u16, v16)
    return (xp.reshape(Bp, Tp, D), xs.reshape(Bs, Ts, D), jnp.stack(st_p), jnp.stack(st_s),
            jnp.stack(kp_rows), jnp.stack(vp_rows), jnp.stack(ks_rows), jnp.stack(vs_rows))
```
